```python
import jax, jax.numpy as jnp
from jax import lax
import numpy as np

D_MODEL = 1024
BATCH = 4
SEQ = 8192
DEPTH = 1
DEC_BATCH = 128
DEC_SEQ = 8
PAST_LEN = 8192
PAGE_SIZE = 128

D_MIX = D_MODEL
D_A = D_MIX // 2
HEAD_DIM_A = 64
N_HEADS_A = D_A // HEAD_DIM_A
DILATED_CONFIGS = ((128, 1), (512, 4), (2048, 16))
WINDOW_MAX = 2048
ATTN_BLOCK = 128
ATTN_SCALE = HEAD_DIM_A ** -0.5
NUM_BUCKETS = 32
REL_MAX_DIST = 2048
D_B = D_MIX - D_A
N_HEADS_B = 4
HEAD_V_B = D_B // N_HEADS_B
EXPAND_B = 128
D_F = N_HEADS_B * EXPAND_B
GLA_CHUNK = 64
IN_WIDTHS = (D_A, D_A, D_A, D_A, D_F, D_F, D_B, D_B)
D_IN = sum(IN_WIDTHS)
IN_SPLITS = [int(s) for s in np.cumsum(IN_WIDTHS)[:-1]]
ALPHA = (2.0 * DEPTH) ** 0.25
BETA = (8.0 * DEPTH) ** -0.25
NORM_EPS = 1e-5

kernel_name = "hymba_dilated_hgrn2_deepnorm_step"


def _rel_buckets(dist):
    max_exact = NUM_BUCKETS // 2
    d = np.maximum(dist, 1).astype(np.float32)
    large = max_exact + (np.log(d / max_exact) / np.log(REL_MAX_DIST / max_exact)
                         * (NUM_BUCKETS - max_exact)).astype(np.int32)
    large = np.minimum(large, NUM_BUCKETS - 1)
    return np.where(dist < max_exact, dist, large).astype(np.int32)


def _rel_bias(table, dist):
    return jnp.moveaxis(table.astype(jnp.float32)[_rel_buckets(dist)], -1, 0)


def _attend(q, k, v, bias, mask):
    s = jnp.einsum('...qhd,...khd->...hqk', q.astype(jnp.float32), k.astype(jnp.float32)) * ATTN_SCALE + bias
    s = jnp.where(mask, s, -jnp.inf)
    m = jnp.max(s, axis=-1, keepdims=True)
    p = jnp.exp(s - m)
    den = jnp.sum(p, axis=-1, keepdims=True)
    out = jnp.einsum('...hqk,...khd->...qhd', p / den, v.astype(jnp.float32))
    lse = jnp.swapaxes((m + jnp.log(den))[..., 0], -1, -2)
    return out, lse


def _combine_by_denominator(outs, lses):
    w = jax.nn.softmax(jnp.stack(lses, axis=0), axis=0)
    return jnp.sum(w[..., None] * jnp.stack(outs, axis=0), axis=0)


def _dilated_prompt_cfg(q, k, v, rel_bias, window, dil):
    B, T, H, D = q.shape
    L = T // dil
    W = window // dil
    BQ = ATTN_BLOCK
    P = (-L) % BQ
    Lp = L + P
    nb = Lp // BQ

    def to_blocks(a):
        a = a.reshape(B, L, dil, H, D).transpose(0, 2, 1, 3, 4)
        a = jnp.pad(a, ((0, 0), (0, 0), (P, 0), (0, 0), (0, 0)))
        return a.reshape(B, dil, nb, BQ, H, D)

    def band(a):
        prev = jnp.pad(a, ((0, 0), (0, 0), (1, 0), (0, 0), (0, 0), (0, 0)))[:, :, :-1]
        return jnp.concatenate([prev, a], axis=3)

    qb = to_blocks(q)
    kb = band(to_blocks(k))
    vb = band(to_blocks(v))
    i = np.arange(BQ)[:, None]
    j = np.arange(2 * BQ)[None, :]
    sub = BQ + i - j
    band_ok = (sub >= 0) & (sub <= W)
    key_sub = (np.arange(nb)[:, None] - 1) * BQ + np.arange(2 * BQ)[None, :]
    q_sub = np.arange(nb)[:, None] * BQ + np.arange(BQ)[None, :]
    mask = band_ok[None] & ((key_sub >= P)[:, None, :] | (q_sub < P)[:, :, None])
    bias = _rel_bias(rel_bias, dil * np.clip(sub, 0, None))
    out, lse = _attend(qb, kb, vb, bias, mask[:, None])
    out = out.reshape(B, dil, Lp, H, D)[:, :, P:].transpose(0, 2, 1, 3, 4).reshape(B, T, H, D)
    lse = lse.reshape(B, dil, Lp, H)[:, :, P:].transpose(0, 2, 1, 3).reshape(B, T, H)
    return out, lse


def _dilated_sample_cfg(q, k_ext, v_ext, rel_bias, window, dil):
    B, Tn, H, D = q.shape
    Tk = k_ext.shape[1]
    W = window // dil
    idx = (Tk - Tn) + np.arange(Tn)[:, None] - dil * np.arange(W + 1)[None, :]
    valid = idx >= 0
    gidx = np.maximum(idx, 0)
    kg = k_ext[:, gidx]
    vg = v_ext[:, gidx]
    bias = _rel_bias(rel_bias, dil * np.arange(W + 1)[None, :])
    out, lse = _attend(q[:, :, None], kg, vg, bias, valid[:, None, None, :])
    return out[:, :, 0], lse[:, :, 0]


def _gla_chunked(q, k, v, g, S0, chunk):
    B, T, H, K = q.shape
    V = v.shape[-1]

    def chunks(a):
        return a.reshape(B, T // chunk, chunk, H, a.shape[-1]).transpose(1, 0, 3, 2, 4)

    causal = jnp.tril(jnp.ones((chunk, chunk), dtype=bool))

    def step(S, inp):
        qc, kc, vc, gc = inp
        b = jnp.cumsum(gc, axis=2)
        inter = jnp.einsum('bhck,bhkv->bhcv', qc * jnp.exp(b), S)
        diff = b[:, :, :, None, :] - b[:, :, None, :, :]
        decay = jnp.exp(jnp.where(causal[:, :, None], diff, -jnp.inf))
        A = jnp.einsum('bhtk,bhsk,bhtsk->bhts', qc, kc, decay)
        intra = jnp.einsum('bhts,bhsv->bhtv', A, vc)
        bC = b[:, :, -1]
        S_new = jnp.exp(bC)[..., None] * S + jnp.einsum(
            'bhsk,bhsv->bhkv', kc * jnp.exp(bC[:, :, None] - b), vc)
        return S_new, inter + intra

    S, o = lax.scan(step, S0, (chunks(q), chunks(k), chunks(v), chunks(g)))
    o = o.transpose(1, 0, 3, 2, 4).reshape(B, T, H, V)
    return o, S


def _hgrn2(qb, fb, ib, S0, lb, chunk):
    B, T, _ = qb.shape
    q = jax.nn.silu(qb.astype(jnp.float32)).reshape(B, T, N_HEADS_B, EXPAND_B) * EXPAND_B ** -0.5
    f = lb + (1.0 - lb) * jax.nn.sigmoid(fb.astype(jnp.float32))
    k = (1.0 - f).reshape(B, T, N_HEADS_B, EXPAND_B)
    g = jnp.log(f).reshape(B, T, N_HEADS_B, EXPAND_B)
    v = ib.astype(jnp.float32).reshape(B, T, N_HEADS_B, HEAD_V_B)
    return _gla_chunked(q, k, v, g, S0.astype(jnp.float32), chunk)


def _project(x, w_in):
    h = jnp.einsum('btd,de->bte', x, w_in)
    return jnp.split(h, IN_SPLITS, axis=-1)


def _heads_a(a):
    B, T, _ = a.shape
    return a.reshape(B, T, N_HEADS_A, HEAD_DIM_A)


def _merge(x, attn, ga, o_b, gb, w_out, norm_g, ln_g, ln_b):
    B, T, _ = x.shape
    a_out = attn.reshape(B, T, D_A).astype(x.dtype) * jax.nn.silu(ga)
    ms = jnp.mean(jnp.square(o_b), axis=-1, keepdims=True)
    o_n = o_b * lax.rsqrt(ms + NORM_EPS) * norm_g.astype(jnp.float32)
    b_out = o_n.reshape(B, T, D_B).astype(x.dtype) * jax.nn.silu(gb)
    mix = jnp.concatenate([a_out, b_out], axis=-1)
    z = (ALPHA * x + jnp.einsum('bte,ed->btd', mix, w_out)).astype(jnp.float32)
    mu = jnp.mean(z, axis=-1, keepdims=True)
    var = jnp.mean(jnp.square(z - mu), axis=-1, keepdims=True)
    y = (z - mu) * lax.rsqrt(var + NORM_EPS) * ln_g.astype(jnp.float32) + ln_b.astype(jnp.float32)
    return y.astype(x.dtype)


def _prompt_layer(x, w_in, w_out, rel_bias, lb, norm_g, ln_g, ln_b):
    B, T, _ = x.shape
    qa, ka, va, ga, qb, fb, ib, gb = _project(x, w_in)
    qa, ka, va = _heads_a(qa), _heads_a(ka), _heads_a(va)
    outs, lses = [], []
    for window, dil in DILATED_CONFIGS:
        o, l = _dilated_prompt_cfg(qa, ka, va, rel_bias, window, dil)
        outs.append(o)
        lses.append(l)
    attn = _combine_by_denominator(outs, lses)
    S0 = jnp.zeros((B, N_HEADS_B, EXPAND_B, HEAD_V_B), jnp.float32)
    o_b, S = _hgrn2(qb, fb, ib, S0, lb, min(GLA_CHUNK, T))
    y = _merge(x, attn, ga, o_b, gb, w_out, norm_g, ln_g, ln_b)
    keep = min(WINDOW_MAX, T)
    return y, ka[:, T - keep:], va[:, T - keep:], S


def _sample_layer(x, k_past, v_past, S0, w_in, w_out, rel_bias, lb, norm_g, ln_g, ln_b):
    B, T, _ = x.shape
    qa, ka, va, ga, qb, fb, ib, gb = _project(x, w_in)
    qa, ka, va = _heads_a(qa), _heads_a(ka), _heads_a(va)
    k_ext = jnp.concatenate([k_past, ka.astype(k_past.dtype)], axis=1)
    v_ext = jnp.concatenate([v_past, va.astype(v_past.dtype)], axis=1)
    outs, lses = [], []
    for window, dil in DILATED_CONFIGS:
        o, l = _dilated_sample_cfg(qa, k_ext, v_ext, rel_bias, window, dil)
        outs.append(o)
        lses.append(l)
    attn = _combine_by_denominator(outs, lses)
    o_b, S = _hgrn2(qb, fb, ib, S0, lb, T)
    y = _merge(x, attn, ga, o_b, gb, w_out, norm_g, ln_g, ln_b)
    wb = k_past.shape[1]
    return y, k_ext[:, T:T + wb], v_ext[:, T:T + wb], S.astype(S0.dtype)


def setup_inputs(seed: int = 0) -> dict:
    key = jax.random.key(seed)
    ks = jax.random.split(key, 12)
    win_buf = min(WINDOW_MAX, PAST_LEN)
    x_prompt = jax.random.normal(ks[0], (BATCH, SEQ, D_MODEL), jnp.float32)
    x_sample = jax.random.normal(ks[1], (DEC_BATCH, DEC_SEQ, D_MODEL), jnp.float32)
    cache_k = jax.random.normal(ks[2], (DEPTH, DEC_BATCH, win_buf, N_HEADS_A, HEAD_DIM_A), jnp.float32)
    cache_v = jax.random.normal(ks[3], (DEPTH, DEC_BATCH, win_buf, N_HEADS_A, HEAD_DIM_A), jnp.float32) * BETA
    state_hgrn = jax.random.normal(ks[4], (DEPTH, DEC_BATCH, N_HEADS_B, EXPAND_B, HEAD_V_B), jnp.float32) * 0.5
    col_scale = jnp.ones((D_IN,), jnp.float32)
    col_scale = col_scale.at[2 * D_A:3 * D_A].set(BETA)
    col_scale = col_scale.at[4 * D_A + 2 * D_F:4 * D_A + 2 * D_F + D_B].set(BETA)
    w_in = jax.random.normal(ks[5], (DEPTH, D_MODEL, D_IN), jnp.float32) * (D_MODEL ** -0.5) * col_scale
    w_out = jax.random.normal(ks[6], (DEPTH, D_MIX, D_MODEL), jnp.float32) * (D_MIX ** -0.5) * BETA
    rel_bias = jax.random.normal(ks[7], (NUM_BUCKETS, N_HEADS_A), jnp.float32) * 0.5
    lb_param = jax.random.normal(ks[8], (DEPTH + 1, D_F), jnp.float32) * 0.5
    hgrn_norm_g = 1.0 + 0.05 * jax.random.normal(ks[9], (DEPTH, HEAD_V_B), jnp.float32)
    ln_g = 1.0 + 0.05 * jax.random.normal(ks[10], (DEPTH, D_MODEL), jnp.float32)
    ln_b = 0.02 * jax.random.normal(ks[11], (DEPTH, D_MODEL), jnp.float32)
    return {"x_prompt": x_prompt, "x_sample": x_sample, "cache_k": cache_k, "cache_v": cache_v,
            "state_hgrn": state_hgrn, "w_in": w_in, "w_out": w_out, "rel_bias": rel_bias,
            "lb_param": lb_param, "hgrn_norm_g": hgrn_norm_g, "ln_g": ln_g, "ln_b": ln_b}


def reference(x_prompt, x_sample, cache_k, cache_v, state_hgrn, w_in, w_out, rel_bias,
              lb_param, hgrn_norm_g, ln_g, ln_b):
    lb_all = jnp.cumsum(jax.nn.softmax(lb_param.astype(jnp.float32), axis=0), axis=0)
    yp, ys = x_prompt, x_sample
    kp_l, vp_l, sp_l, ks_l, vs_l, ss_l = [], [], [], [], [], []
    for l in range(DEPTH):
        yp, kp, vp, sp = _prompt_layer(yp, w_in[l], w_out[l], rel_bias, lb_all[l],
                                       hgrn_norm_g[l], ln_g[l], ln_b[l])
        ys, k_s, v_s, s_s = _sample_layer(ys, cache_k[l], cache_v[l], state_hgrn[l], w_in[l], w_out[l],
                                          rel_bias, lb_all[l], hgrn_norm_g[l], ln_g[l], ln_b[l])
        kp_l.append(kp)
        vp_l.append(vp)
        sp_l.append(sp)
        ks_l.append(k_s)
        vs_l.append(v_s)
        ss_l.append(s_s)
    return (yp, ys, jnp.stack(kp_l), jnp.stack(vp_l), jnp.stack(ks_l), jnp.stack(vs_l),
            jnp.stack(sp_l), jnp.stack(ss_l))
```

```python
import functools

import numpy as np
import jax
import jax.numpy as jnp
from jax import lax
from jax.experimental import pallas as pl
from jax.experimental.pallas import tpu as pltpu

F32 = jnp.float32
BF16 = jnp.bfloat16

D_MODEL = 1024
D_A = 512
HEAD_DIM_A = 64
N_HEADS_A = 8
DILATIONS = (1, 4, 16)
WIN_STEPS = 128
WINDOW_MAX = 2048
ATTN_BLOCK = 128
ATTN_SCALE = HEAD_DIM_A ** -0.5
NUM_BUCKETS = 32
REL_MAX_DIST = 2048
D_B = 512
N_HEADS_B = 4
HEAD_V_B = 128
EXPAND_B = 128
D_F = 512
GLA_CHUNK = 64
D_IN = 4 * D_A + 2 * D_F + 2 * D_B
DEPTH = 1
ALPHA = (2.0 * DEPTH) ** 0.25
NORM_EPS = 1e-5
DEC_SEQ = 8

LANES = 128
MIB = 1024 * 1024

NT_DIMS = (((1,), (1,)), ((), ()))
TN_DIMS = (((0,), (0,)), ((), ()))


def _cparams(semantics, vmem_mib):
    return pltpu.CompilerParams(dimension_semantics=semantics, vmem_limit_bytes=vmem_mib * MIB)


def _rel_buckets(dist):
    max_exact = NUM_BUCKETS // 2
    d = np.maximum(dist, 1).astype(np.float32)
    large = max_exact + (np.log(d / max_exact) / np.log(REL_MAX_DIST / max_exact)
                         * (NUM_BUCKETS - max_exact)).astype(np.int32)
    large = np.minimum(large, NUM_BUCKETS - 1)
    return np.where(dist < max_exact, dist, large).astype(np.int32)


def _split3(x):
    hi = x.astype(BF16)
    r1 = x - hi.astype(F32)
    mid = r1.astype(BF16)
    lo = (r1 - mid.astype(F32)).astype(BF16)
    return hi, mid, lo


def _dot(a, b):
    return jnp.dot(a, b, preferred_element_type=F32)


def _dot3(mat_bf16, x):
    hi, mid, lo = _split3(x)
    return _dot(mat_bf16, hi) + _dot(mat_bf16, mid) + _dot(mat_bf16, lo)


def _dot2_rhs(x, mat_bf16):
    hi = x.astype(BF16)
    lo = (x - hi.astype(F32)).astype(BF16)
    return _dot(hi, mat_bf16) + _dot(lo, mat_bf16)


def _silu(x):
    return x * jax.nn.sigmoid(x)


def _bcast_rows(x, block, j):
    rows = x.shape[0]
    parts = [jnp.broadcast_to(x[r0 + j:r0 + j + 1, :], (block, x.shape[1])) for r0 in range(0, rows, block)]
    return parts[0] if len(parts) == 1 else jnp.concatenate(parts, axis=0)


def _proj_body(x_ref, w_ref, *out_refs, groups):
    x = x_ref[...].astype(BF16)
    for (lo, width, scale, idxs) in groups:
        h = _dot(x, w_ref[:, lo:lo + width])
        if scale != 1.0:
            h = h * scale
        for i in idxs:
            out_refs[i][...] = h.astype(out_refs[i].dtype)


def _proj(x2d, w_bf16, outs, tm):
    n = x2d.shape[0]
    groups = {}
    for i, (lo, width, scale, _, _, _) in enumerate(outs):
        groups.setdefault((lo, width, scale), []).append(i)
    groups = tuple((lo, width, scale, tuple(idxs)) for (lo, width, scale), idxs in groups.items())
    out_shape = [jax.ShapeDtypeStruct((rows, width), dt) for (_, width, _, dt, rows, _) in outs]
    out_specs = [pl.BlockSpec((tm, width), imap) for (_, width, _, _, _, imap) in outs]
    return pl.pallas_call(
        functools.partial(_proj_body, groups=groups),
        grid=(n // tm,),
        in_specs=[pl.BlockSpec((tm, D_MODEL), lambda i: (i, 0)),
                  pl.BlockSpec((D_MODEL, D_IN), lambda i: (0, 0))],
        out_specs=out_specs,
        out_shape=out_shape,
        compiler_params=_cparams(("arbitrary",), 56),
        name="proj",
    )(x2d, w_bf16)


def _attn_prompt_body(q_ref, k_ref, v_ref, bias_ref, o_ref, lse_ref, kp_ref, vp_ref):
    n = pl.program_id(2)

    @pl.when(n == 0)
    def _():
        kp_ref[...] = jnp.zeros_like(kp_ref)
        vp_ref[...] = jnp.zeros_like(vp_ref)

    first = n == 0
    for h in range(N_HEADS_A):
        sl = slice(HEAD_DIM_A * h, HEAD_DIM_A * (h + 1))
        qh = q_ref[:, sl]
        s0 = lax.dot_general(qh, kp_ref[:, sl], NT_DIMS, preferred_element_type=F32) + bias_ref[h, :, 0:ATTN_BLOCK]
        s0 = jnp.where(first, -jnp.inf, s0)
        s1 = lax.dot_general(qh, k_ref[:, sl], NT_DIMS, preferred_element_type=F32) + bias_ref[h, :, ATTN_BLOCK:]
        m = jnp.maximum(jnp.max(s0, axis=-1, keepdims=True), jnp.max(s1, axis=-1, keepdims=True))
        p0 = jnp.exp(s0 - m)
        p1 = jnp.exp(s1 - m)
        den = jnp.sum(p0, axis=-1, keepdims=True) + jnp.sum(p1, axis=-1, keepdims=True)
        acc = _dot(p0.astype(BF16), vp_ref[:, sl]) + _dot(p1.astype(BF16), v_ref[:, sl])
        o_ref[:, sl] = (acc / den).astype(o_ref.dtype)
        lse_ref[:, h:h + 1] = m + jnp.log(den)
    kp_ref[...] = k_ref[...]
    vp_ref[...] = v_ref[...]


def _attn_prompt(q, k, v, bias, dil):
    b, t, _ = q.shape
    length = t // dil
    nb = length // ATTN_BLOCK
    view = lambda a: a.reshape(b, length, dil * D_A)
    blk = pl.BlockSpec((None, ATTN_BLOCK, D_A), lambda bi, r, n: (bi, n, r))
    o, lse = pl.pallas_call(
        _attn_prompt_body,
        grid=(b, dil, nb),
        in_specs=[blk, blk, blk,
                  pl.BlockSpec((N_HEADS_A, ATTN_BLOCK, 2 * ATTN_BLOCK), lambda bi, r, n: (0, 0, 0))],
        out_specs=[blk,
                   pl.BlockSpec((None, None, ATTN_BLOCK, N_HEADS_A), lambda bi, r, n: (bi, r, n, 0))],
        out_shape=[jax.ShapeDtypeStruct((b, length, dil * D_A), BF16),
                   jax.ShapeDtypeStruct((b, dil, length, N_HEADS_A), F32)],
        scratch_shapes=[pltpu.VMEM((ATTN_BLOCK, D_A), BF16), pltpu.VMEM((ATTN_BLOCK, D_A), BF16)],
        compiler_params=_cparams(("arbitrary", "arbitrary", "arbitrary"), 32),
        name=f"attn_prompt_d{dil}",
    )(view(q), view(k), view(v), bias)
    return o.reshape(b * t, D_A), lse


def _lower_bound(lbp_ref):
    p0 = lbp_ref[0:1, :]
    p1 = lbp_ref[1:2, :]
    m = jnp.maximum(p0, p1)
    e0 = jnp.exp(p0 - m)
    e1 = jnp.exp(p1 - m)
    return e0 / (e0 + e1)


def _gates(qb, fb, lb):
    f = lb + (1.0 - lb) * jax.nn.sigmoid(fb)
    return _silu(qb) * (EXPAND_B ** -0.5), 1.0 - f, jnp.log(f)


def _iota2(shape, dim):
    return lax.broadcasted_iota(jnp.int32, shape, dim)


def _div2(x, n):
    return jnp.right_shift(x, int(n).bit_length() - 1)


def _mod2(x, n):
    return jnp.bitwise_and(x, n - 1)


def _head_expand_matrix():
    return (_iota2((LANES, D_A), 0) == _div2(_iota2((LANES, D_A), 1), HEAD_DIM_A)).astype(BF16)


def _diag_block_scores(q, kk, b, a_heads, row, col, block):
    same = _div2(row, block) == _div2(col, block)
    for j in range(block):
        bj = _bcast_rows(b, block, j)
        kj = _bcast_rows(kk, block, j)
        x = q * jnp.exp(jnp.minimum(b - bj, 0.0)) * kj
        sel = same & (_mod2(col, block) == j) & (_mod2(row, block) >= j)
        for h in range(N_HEADS_B):
            cs = jnp.sum(x[:, h * EXPAND_B:(h + 1) * EXPAND_B], axis=-1, keepdims=True)
            a_heads[h] = jnp.where(sel, cs, a_heads[h])
    return a_heads


def _hgrn_prompt_body(qb_ref, fb_ref, ib_ref, lbp_ref, o_ref, s_ref, st_ref, *, n_chunks):
    t = pl.program_id(1)
    c_len = GLA_CHUNK

    @pl.when(t == 0)
    def _():
        st_ref[...] = jnp.zeros_like(st_ref)

    lb = _lower_bound(lbp_ref)
    row = _iota2((c_len, c_len), 0)
    col = _iota2((c_len, c_len), 1)
    tri = (col <= row).astype(BF16)
    levels = []
    for n in (32, 16, 8):
        mask = ((_div2(row, 2 * n) == _div2(col, 2 * n)) & (_mod2(_div2(row, n), 2) == 1)
                & (_mod2(_div2(col, n), 2) == 0))
        levels.append((n, mask))

    def chunk(c, carry):
        r0 = pl.multiple_of(c * c_len, c_len)
        rows = pl.ds(r0, c_len)
        q, kk, g = _gates(qb_ref[rows, :], fb_ref[rows, :], lb)
        v = ib_ref[rows, :]
        b = _dot3(tri, g)
        b_end = b[c_len - 1:c_len, :]
        q_in = (q * jnp.exp(b)).astype(BF16)
        k_out = (kk * jnp.exp(b_end - b)).astype(BF16)
        v16 = v.astype(BF16)
        dec = jnp.exp(b_end)

        a_heads = [jnp.zeros((c_len, c_len), F32) for _ in range(N_HEADS_B)]
        for n, mask in levels:
            beta = _bcast_rows(b, 2 * n, n - 1)
            ql = (q * jnp.exp(jnp.minimum(b - beta, 0.0))).astype(BF16)
            kl = (kk * jnp.exp(jnp.minimum(beta - b, 0.0))).astype(BF16)
            for h in range(N_HEADS_B):
                hs = slice(h * EXPAND_B, (h + 1) * EXPAND_B)
                part = lax.dot_general(ql[:, hs], kl[:, hs], NT_DIMS, preferred_element_type=F32)
                a_heads[h] = a_heads[h] + jnp.where(mask, part, 0.0)
        a_heads = _diag_block_scores(q, kk, b, a_heads, row, col, 8)

        for h in range(N_HEADS_B):
            hs = slice(h * EXPAND_B, (h + 1) * EXPAND_B)
            st = st_ref[h]
            inter = lax.dot_general(q_in[:, hs], st.astype(BF16), NT_DIMS, preferred_element_type=F32)
            intra = _dot(a_heads[h].astype(BF16), v16[:, hs])
            o_ref[rows, hs] = inter + intra
            upd = lax.dot_general(v16[:, hs], k_out[:, hs], TN_DIMS, preferred_element_type=F32)
            st_ref[h] = st * dec[:, hs] + upd
        return carry

    lax.fori_loop(0, n_chunks, chunk, 0)

    @pl.when(t == pl.num_programs(1) - 1)
    def _():
        for h in range(N_HEADS_B):
            s_ref[h] = st_ref[h].T


def _hgrn_prompt(qb, fb, ib, lb_param, b, t, tt):
    nt = t // tt
    blk = pl.BlockSpec((tt, D_F), lambda bi, ti: (bi * nt + ti, 0))
    return pl.pallas_call(
        functools.partial(_hgrn_prompt_body, n_chunks=tt // GLA_CHUNK),
        grid=(b, nt),
        in_specs=[blk, blk, blk, pl.BlockSpec((DEPTH + 1, D_F), lambda bi, ti: (0, 0))],
        out_specs=[blk,
                   pl.BlockSpec((None, N_HEADS_B, EXPAND_B, HEAD_V_B), lambda bi, ti: (bi, 0, 0, 0))],
        out_shape=[jax.ShapeDtypeStruct((b * t, D_B), F32),
                   jax.ShapeDtypeStruct((b, N_HEADS_B, EXPAND_B, HEAD_V_B), F32)],
        scratch_shapes=[pltpu.VMEM((N_HEADS_B, HEAD_V_B, EXPAND_B), F32)],
        compiler_params=_cparams(("arbitrary", "arbitrary"), 32),
        name="hgrn_prompt",
    )(qb, fb, ib, lb_param)


def _hgrn_sample_body(qb_ref, fb_ref, ib_ref, lbp_ref, s0_ref, o_ref, s_ref, *, nbatch):
    rows_n = nbatch * DEC_SEQ
    assert rows_n == LANES
    lb = _lower_bound(lbp_ref)
    row = _iota2((rows_n, rows_n), 0)
    col = _iota2((rows_n, rows_n), 1)
    same = _div2(row, DEC_SEQ) == _div2(col, DEC_SEQ)
    tri = (same & (col <= row)).astype(BF16)
    pick = (col == row * DEC_SEQ + (DEC_SEQ - 1)).astype(BF16)

    q, kk, g = _gates(qb_ref[...], fb_ref[...], lb)
    v = ib_ref[...]
    b = _dot3(tri, g)
    b_end = _bcast_rows(b, DEC_SEQ, DEC_SEQ - 1)
    q_in = (q * jnp.exp(b)).astype(BF16)
    k_out = kk * jnp.exp(b_end - b)
    v16 = v.astype(BF16)

    a_heads = [jnp.zeros((rows_n, rows_n), F32) for _ in range(N_HEADS_B)]
    a_heads = _diag_block_scores(q, kk, b, a_heads, row, col, DEC_SEQ)

    rsel = _div2(_iota2((rows_n, 1), 0), DEC_SEQ)
    for h in range(N_HEADS_B):
        hs = slice(h * EXPAND_B, (h + 1) * EXPAND_B)
        intra = _dot(a_heads[h].astype(BF16), v16[:, hs])
        dec_t = jnp.exp(_dot3(pick, b[:, hs])).T
        k_out_t = k_out[:, hs].T.astype(BF16)
        for bi in range(nbatch):
            rs = slice(bi * DEC_SEQ, (bi + 1) * DEC_SEQ)
            s0 = s0_ref[bi, h]
            inter = _dot(q_in[rs, hs], s0.astype(BF16))
            o_ref[rs, hs] = inter + intra[rs, :]
            v_b = jnp.where(rsel == bi, v16[:, hs], jnp.zeros_like(v16[:, hs]))
            upd = _dot(k_out_t, v_b)
            s_ref[bi, h] = s0 * dec_t[:, bi:bi + 1] + upd


def _hgrn_sample(qb, fb, ib, lb_param, s0, nbatch):
    b = s0.shape[0]
    rows_n = nbatch * DEC_SEQ
    blk = pl.BlockSpec((rows_n, D_F), lambda i: (i, 0))
    sblk = pl.BlockSpec((nbatch, N_HEADS_B, EXPAND_B, HEAD_V_B), lambda i: (i, 0, 0, 0))
    return pl.pallas_call(
        functools.partial(_hgrn_sample_body, nbatch=nbatch),
        grid=(b // nbatch,),
        in_specs=[blk, blk, blk, pl.BlockSpec((DEPTH + 1, D_F), lambda i: (0, 0)), sblk],
        out_specs=[blk, sblk],
        out_shape=[jax.ShapeDtypeStruct((b * DEC_SEQ, D_B), F32),
                   jax.ShapeDtypeStruct(s0.shape, F32)],
        compiler_params=_cparams(("arbitrary",), 32),
        name="hgrn_sample",
    )(qb, fb, ib, lb_param, s0)


SLAB_CACHE_ROWS = 128
SLAB_ROWS = SLAB_CACHE_ROWS + DEC_SEQ
POS_PER_ROW = 16


def _sample_slab_distances():
    dist = -np.ones((len(DILATIONS), DEC_SEQ, SLAB_ROWS), np.int64)
    for ci, dil in enumerate(DILATIONS):
        for t in range(DEC_SEQ):
            for i in range(SLAB_CACHE_ROWS):
                if dil == 16:
                    p = POS_PER_ROW * i + t
                elif dil == 4:
                    u, mm = divmod(i, 32)
                    p = POS_PER_ROW * (96 + mm) + (t % 4) + 4 * u
                else:
                    r, mm = divmod(i, 8)
                    p = POS_PER_ROW * (120 + mm) + r
                d = WINDOW_MAX + t - p
                if d % dil == 0 and 0 < d <= WIN_STEPS * dil:
                    dist[ci, t, i] = d
            for t2 in range(DEC_SEQ):
                d = t - t2
                if d >= 0 and d % dil == 0:
                    dist[ci, t, SLAB_CACHE_ROWS + t2] = d
    return dist


def _slab(c_ref, n_ref, dil, t):
    if dil == 16:
        parts = [c_ref[:, D_A * t:D_A * (t + 1)]]
    elif dil == 4:
        parts = [c_ref[96:128, D_A * (t % 4 + 4 * u):D_A * (t % 4 + 4 * u + 1)] for u in range(4)]
    else:
        parts = [c_ref[120:128, D_A * r:D_A * (r + 1)] for r in range(POS_PER_ROW)]
    return jnp.concatenate(parts + [n_ref[...]], axis=0)


def _attn_sample_body(q_ref, kn_ref, vn_ref, ck_ref, cv_ref, bias_ref, attn_ref, ok_ref, ov_ref):
    half = (POS_PER_ROW // 2) * D_A
    for c_ref, n_ref, o_ref in ((ck_ref, kn_ref, ok_ref), (cv_ref, vn_ref, ov_ref)):
        o_ref[:, 0:half] = c_ref[:, half:]
        o_ref[0:SLAB_CACHE_ROWS - 1, half:] = c_ref[1:SLAB_CACHE_ROWS, 0:half]
        for t in range(DEC_SEQ):
            o_ref[SLAB_CACHE_ROWS - 1:SLAB_CACHE_ROWS, half + D_A * t:half + D_A * (t + 1)] = n_ref[t:t + 1, :]

    lane_r = _iota2((D_A, LANES), 0)
    lane_c = _iota2((D_A, LANES), 1)
    e_mat = (_div2(lane_r, HEAD_DIM_A) == lane_c).astype(BF16)
    et_mat = _head_expand_matrix()

    for t in range(DEC_SEQ):
        qt = q_ref[t:t + 1, :] * ATTN_SCALE
        scores, lses = [], []
        for ci, dil in enumerate(DILATIONS):
            prod = (_slab(ck_ref, kn_ref, dil, t) * qt).astype(BF16)
            s = _dot(prod, e_mat) + bias_ref[ci * DEC_SEQ + t]
            m = jnp.max(s, axis=0, keepdims=True)
            p = jnp.exp(s - m)
            den = jnp.sum(p, axis=0, keepdims=True)
            scores.append(p / den)
            lses.append(m + jnp.log(den))
        lmax = jnp.maximum(jnp.maximum(lses[0], lses[1]), lses[2])
        ws = [jnp.exp(l - lmax) for l in lses]
        wsum = ws[0] + ws[1] + ws[2]
        acc = jnp.zeros((SLAB_ROWS, D_A), F32)
        for ci, dil in enumerate(DILATIONS):
            pw = (scores[ci] * (ws[ci] / wsum)).astype(BF16)
            acc = acc + _dot(pw, et_mat) * _slab(cv_ref, vn_ref, dil, t)
        attn_ref[t:t + 1, :] = jnp.sum(acc, axis=0, keepdims=True)


def _attn_sample(q, kn, vn, cache_k, cache_v, bias):
    b = cache_k.shape[0]
    rows = WINDOW_MAX // POS_PER_ROW
    width = POS_PER_ROW * D_A
    view = lambda a: a.reshape(b, rows, width)
    small = pl.BlockSpec((DEC_SEQ, D_A), lambda i: (i, 0))
    big = pl.BlockSpec((None, rows, width), lambda i: (i, 0, 0))
    attn, ok, ov = pl.pallas_call(
        _attn_sample_body,
        grid=(b,),
        in_specs=[small, small, small, big, big,
                  pl.BlockSpec(bias.shape, lambda i: (0, 0, 0))],
        out_specs=[small, big, big],
        out_shape=[jax.ShapeDtypeStruct((b * DEC_SEQ, D_A), F32),
                   jax.ShapeDtypeStruct((b, rows, width), F32),
                   jax.ShapeDtypeStruct((b, rows, width), F32)],
        compiler_params=_cparams(("arbitrary",), 56),
        name="attn_sample",
    )(q, kn, vn, view(cache_k), view(cache_v), bias)
    return attn, ok, ov


def _merge_body(*refs, n_cfg):
    x_ref = refs[0]
    o_refs = refs[1:1 + n_cfg]
    l_refs = refs[1 + n_cfg:1 + 2 * n_cfg] if n_cfg > 1 else ()
    ga_ref, ob_ref, gb_ref, wout_ref, ng_ref, lg_ref, lbias_ref, y_ref = refs[1 + n_cfg + len(l_refs):]

    if n_cfg > 1:
        et_mat = _head_expand_matrix()
        ls =[r[...] for r in l_refs]
        lmax = functools.reduce(jnp.maximum, ls)
        ws = [jnp.exp(l - lmax) for l in ls]
        wsum = functools.reduce(lambda a, c: a + c, ws)
        attn = None
        for w, o_ref in zip(ws, o_refs):
            term = _dot2_rhs(w / wsum, et_mat) * o_ref[...].astype(F32)
            attn = term if attn is None else attn + term
    else:
        attn = o_refs[0][...].astype(F32)

    a_out = attn * _silu(ga_ref[...])
    parts = [a_out.astype(BF16)]
    for h in range(N_HEADS_B):
        hs = slice(h * HEAD_V_B, (h + 1) * HEAD_V_B)
        oh = ob_ref[:, hs]
        ms = jnp.mean(oh * oh, axis=-1, keepdims=True)
        on = oh * lax.rsqrt(ms + NORM_EPS) * ng_ref[...]
        parts.append((on * _silu(gb_ref[:, hs])).astype(BF16))
    mix = jnp.concatenate(parts, axis=-1)
    z = ALPHA * x_ref[...] + _dot(mix, wout_ref[...])
    mu = jnp.mean(z, axis=-1, keepdims=True)
    zc = z - mu
    var = jnp.mean(zc * zc, axis=-1, keepdims=True)
    y_ref[...] = zc * lax.rsqrt(var + NORM_EPS) * lg_ref[...] + lbias_ref[...]


def _merge(x2d, attn_outs, lses, ga, ob, gb, w_out_bf16, norm_g, ln_g, ln_b, tm):
    n = x2d.shape[0]
    n_cfg = len(attn_outs)
    row = lambda width: pl.BlockSpec((tm, width), lambda i: (i, 0))
    const = lambda shape: pl.BlockSpec(shape, lambda i: (0,) * len(shape))
    in_specs = ([row(D_MODEL)] + [row(D_A)] * n_cfg + [row(LANES)] * len(lses)
                + [row(D_A), row(D_B), row(D_B), const((D_MODEL, D_MODEL)),
                   const((1, HEAD_V_B)), const((1, D_MODEL)), const((1, D_MODEL))])
    return pl.pallas_call(
        functools.partial(_merge_body, n_cfg=n_cfg),
        grid=(n // tm,),
        in_specs=in_specs,
        out_specs=row(D_MODEL),
        out_shape=jax.ShapeDtypeStruct((n, D_MODEL), F32),
        compiler_params=_cparams(("arbitrary",), 48),
        name=f"merge_{n_cfg}",
    )(x2d, *attn_outs, *lses, ga, ob, gb, w_out_bf16, norm_g, ln_g, ln_b)


def _prompt_bias(dist_table, dil):
    i = np.arange(ATTN_BLOCK)[:, None]
    j = np.arange(2 * ATTN_BLOCK)[None, :]
    sub = ATTN_BLOCK + i - j
    ok = (sub >= 0) & (sub <= WIN_STEPS)
    bias = dist_table[dil * np.clip(sub, 0, WIN_STEPS)]
    bias = jnp.where(ok[:, :, None], bias, -jnp.inf)
    return jnp.transpose(bias, (2, 0, 1))


def _sample_bias(dist_table):
    dist = _sample_slab_distances()
    bias = jnp.where((dist >= 0)[..., None], dist_table[np.maximum(dist, 0)], -jnp.inf)
    bias = bias.reshape(len(DILATIONS) * DEC_SEQ, SLAB_ROWS, N_HEADS_A)
    return jnp.pad(bias, ((0, 0), (0, 0), (0, LANES - N_HEADS_A)))


def kernel(x_prompt, x_sample, cache_k, cache_v, state_hgrn, w_in, w_out, rel_bias, lb_param, hgrn_norm_g, ln_g, ln_b):
    bsz, seq, _ = x_prompt.shape
    dec_b, dec_t, _ = x_sample.shape
    assert dec_t == DEC_SEQ and cache_k.shape[2] == WINDOW_MAX and w_in.shape[0] == DEPTH
    keep = min(WINDOW_MAX, seq)

    w_in16 = w_in[0].astype(BF16)
    w_out16 = w_out[0].astype(BF16)
    dist_table = rel_bias.astype(F32)[_rel_buckets(np.arange(WINDOW_MAX + 1))]
    norm_g = hgrn_norm_g[0].reshape(1, HEAD_V_B).astype(F32)
    lng = ln_g[0].reshape(1, D_MODEL).astype(F32)
    lnb = ln_b[0].reshape(1, D_MODEL).astype(F32)
    lbp = lb_param.astype(F32)

    tm = 512
    n_p = bsz * seq
    tiles_per_seq = seq // tm
    win_tiles = keep // tm
    main = lambda i: (i, 0)
    win = lambda i: ((i // tiles_per_seq) * win_tiles + jnp.maximum(i % tiles_per_seq - (tiles_per_seq - win_tiles), 0), 0)
    o_qa, o_ka, o_va, o_ga, o_qb, o_fb, o_ib, o_gb = (D_A * 0, D_A * 1, D_A * 2, D_A * 3, 4 * D_A,
                                                    4 * D_A + D_F, 4 * D_A + 2 * D_F, 4 * D_A + 2 * D_F + D_B)
    outs = [
        (o_qa, D_A, ATTN_SCALE, BF16, n_p, main),
        (o_ka, D_A, 1.0, BF16, n_p, main),
        (o_va, D_A, 1.0, BF16, n_p, main),
        (o_ka, D_A, 1.0, F32, bsz * keep, win),
        (o_va, D_A, 1.0, F32, bsz * keep, win),
        (o_ga, D_A, 1.0, F32, n_p, main),
        (o_qb, D_F, 1.0, F32, n_p, main),
        (o_fb, D_F, 1.0, F32, n_p, main),
        (o_ib, D_B, 1.0, F32, n_p, main),
        (o_gb, D_B, 1.0, F32, n_p, main),
    ]
    x2d = x_prompt.reshape(n_p, D_MODEL)
    qa, ka, va, kwin, vwin, ga, qb, fb, ib, gb = _proj(x2d, w_in16, outs, tm)

    attn_outs, lses = [], []
    for dil in DILATIONS:
        o, lse = _attn_prompt(qa.reshape(bsz, seq, D_A), ka.reshape(bsz, seq, D_A), va.reshape(bsz, seq, D_A),
                              _prompt_bias(dist_table, dil), dil)
        lse = jnp.transpose(lse, (0, 2, 1, 3)).reshape(n_p, N_HEADS_A)
        attn_outs.append(o)
        lses.append(jnp.pad(lse, ((0, 0), (0, LANES - N_HEADS_A))))
    ob, s_prompt = _hgrn_prompt(qb, fb, ib, lbp, bsz, seq, 512)
    y_prompt = _merge(x2d, attn_outs, lses, ga, ob, gb, w_out16, norm_g, lng, lnb, tm).reshape(bsz, seq, D_MODEL)

    n_s = dec_b * dec_t
    outs_s = [(lo, D_A, 1.0, F32, n_s, main) for lo in (o_qa, o_ka, o_va, o_ga, o_qb, o_fb, o_ib, o_gb)]
    xs2d = x_sample.reshape(n_s, D_MODEL)
    tm_s = min(tm, n_s)
    sqa, ska, sva, sga, sqb, sfb, sib, sgb = _proj(xs2d, w_in16, outs_s, tm_s)
    attn_s, k_new_win, v_new_win = _attn_sample(sqa, ska, sva, cache_k[0].reshape(dec_b, WINDOW_MAX, D_A),
                                                cache_v[0].reshape(dec_b, WINDOW_MAX, D_A), _sample_bias(dist_table))
    ob_s, s_sample = _hgrn_sample(sqb, sfb, sib, lbp, state_hgrn[0], LANES // DEC_SEQ)
    y_sample = _merge(xs2d, [attn_s], [], sga, ob_s, sgb, w_out16, norm_g, lng, lnb, tm_s).reshape(dec_b, dec_t, D_MODEL)

    win_shape = (DEPTH, bsz, keep, N_HEADS_A, HEAD_DIM_A)
    cache_shape = (DEPTH, dec_b, WINDOW_MAX, N_HEADS_A, HEAD_DIM_A)
    return (y_prompt, y_sample,
            kwin.reshape(win_shape), vwin.reshape(win_shape),
            k_new_win.reshape(cache_shape), v_new_win.reshape(cache_shape),
            s_prompt[None], s_sample[None].astype(state_hgrn.dtype))
```

```python
import functools

import numpy as np
import jax
import jax.numpy as jnp
from jax import lax
from jax.experimental import pallas as pl
from jax.experimental.pallas import tpu as pltpu

F32 = jnp.float32
BF16 = jnp.bfloat16

D_MODEL = 1024
D_A = 512
HEAD_DIM_A = 64
N_HEADS_A = 8
DILATIONS = (1, 4, 16)
WIN_STEPS = 128
WINDOW_MAX = 2048
ATTN_BLOCK = 128
ATTN_SCALE = HEAD_DIM_A ** -0.5
NUM_BUCKETS = 32
REL_MAX_DIST = 2048
D_B = 512
N_HEADS_B = 4
HEAD_V_B = 128
EXPAND_B = 128
D_F = 512
GLA_CHUNK = 64
D_IN = 4 * D_A + 2 * D_F + 2 * D_B
DEPTH = 1
ALPHA = (2.0 * DEPTH) ** 0.25
NORM_EPS = 1e-5
DEC_SEQ = 8

LANES = 128
MIB = 1024 * 1024

NT_DIMS = (((1,), (1,)), ((), ()))
TN_DIMS = (((0,), (0,)), ((), ()))


def _cparams(semantics, vmem_mib):
    return pltpu.CompilerParams(dimension_semantics=semantics, vmem_limit_bytes=vmem_mib * MIB)


def _rel_buckets(dist):
    max_exact = NUM_BUCKETS // 2
    d = np.maximum(dist, 1).astype(np.float32)
    large = max_exact + (np.log(d / max_exact) / np.log(REL_MAX_DIST / max_exact)
                         * (NUM_BUCKETS - max_exact)).astype(np.int32)
    large = np.minimum(large, NUM_BUCKETS - 1)
    return np.where(dist < max_exact, dist, large).astype(np.int32)


def _split3(x):
    hi = x.astype(BF16)
    r1 = x - hi.astype(F32)
    mid = r1.astype(BF16)
    lo = (r1 - mid.astype(F32)).astype(BF16)
    return hi, mid, lo


def _dot(a, b):
    return jnp.dot(a, b, preferred_element_type=F32)


def _dot3(mat_bf16, x):
    hi, mid, lo = _split3(x)
    return _dot(mat_bf16, hi) + _dot(mat_bf16, mid) + _dot(mat_bf16, lo)


def _dot2_rhs(x, mat_bf16):
    hi = x.astype(BF16)
    lo = (x - hi.astype(F32)).astype(BF16)
    return _dot(hi, mat_bf16) + _dot(lo, mat_bf16)


def _silu(x):
    return x * jax.nn.sigmoid(x)


def _bcast_rows(x, block, j):
    rows = x.shape[0]
    parts = [jnp.broadcast_to(x[r0 + j:r0 + j + 1, :], (block, x.shape[1])) for r0 in range(0, rows, block)]
    return parts[0] if len(parts) == 1 else jnp.concatenate(parts, axis=0)


def _proj_body(x_ref, w_ref, *out_refs, groups):
    x = x_ref[...].astype(BF16)
    for (lo, width, scale, idxs) in groups:
        h = _dot(x, w_ref[:, lo:lo + width])
        if scale != 1.0:
            h = h * scale
        for i in idxs:
            out_refs[i][...] = h.astype(out_refs[i].dtype)


def _proj(x2d, w_bf16, outs, tm):
    n = x2d.shape[0]
    groups = {}
    for i, (lo, width, scale, _, _, _) in enumerate(outs):
        groups.setdefault((lo, width, scale), []).append(i)
    groups = tuple((lo, width, scale, tuple(idxs)) for (lo, width, scale), idxs in groups.items())
    out_shape = [jax.ShapeDtypeStruct((rows, width), dt) for (_, width, _, dt, rows, _) in outs]
    out_specs = [pl.BlockSpec((tm, width), imap) for (_, width, _, _, _, imap) in outs]
    return pl.pallas_call(
        functools.partial(_proj_body, groups=groups),
        grid=(n // tm,),
        in_specs=[pl.BlockSpec((tm, D_MODEL), lambda i: (i, 0)),
                  pl.BlockSpec((D_MODEL, D_IN), lambda i: (0, 0))],
        out_specs=out_specs,
        out_shape=out_shape,
        compiler_params=_cparams(("arbitrary",), 56),
        name="proj",
    )(x2d, w_bf16)


def _attn_prompt_body(q_ref, k_ref, v_ref, bias_ref, o_ref, lse_ref, kp_ref, vp_ref):
    n = pl.program_id(2)

    @pl.when(n == 0)
    def _():
        kp_ref[...] = jnp.zeros_like(kp_ref)
        vp_ref[...] = jnp.zeros_like(vp_ref)

    first = n == 0
    for h in range(N_HEADS_A):
        sl = slice(HEAD_DIM_A * h, HEAD_DIM_A * (h + 1))
        qh = q_ref[:, sl]
        s0 = lax.dot_general(qh, kp_ref[:, sl], NT_DIMS, preferred_element_type=F32) + bias_ref[h, :, 0:ATTN_BLOCK]
        s0 = jnp.where(first, -jnp.inf, s0)
        s1 = lax.dot_general(qh, k_ref[:, sl], NT_DIMS, preferred_element_type=F32) + bias_ref[h, :, ATTN_BLOCK:]
        m = jnp.maximum(jnp.max(s0, axis=-1, keepdims=True), jnp.max(s1, axis=-1, keepdims=True))
        p0 = jnp.exp(s0 - m)
        p1 = jnp.exp(s1 - m)
        den = jnp.sum(p0, axis=-1, keepdims=True) + jnp.sum(p1, axis=-1, keepdims=True)
        acc = _dot(p0.astype(BF16), vp_ref[:, sl]) + _dot(p1.astype(BF16), v_ref[:, sl])
        o_ref[:, sl] = (acc / den).astype(o_ref.dtype)
        lse_ref[:, h:h + 1] = m + jnp.log(den)
    kp_ref[...] = k_ref[...]
    vp_ref[...] = v_ref[...]


def _attn_prompt(q, k, v, bias, dil):
    b, t, _ = q.shape
    length = t // dil
    nb = length // ATTN_BLOCK
    view = lambda a: a.reshape(b, length, dil * D_A)
    blk = pl.BlockSpec((None, ATTN_BLOCK, D_A), lambda bi, r, n: (bi, n, r))
    o, lse = pl.pallas_call(
        _attn_prompt_body,
        grid=(b, dil, nb),
        in_specs=[blk, blk, blk,
                  pl.BlockSpec((N_HEADS_A, ATTN_BLOCK, 2 * ATTN_BLOCK), lambda bi, r, n: (0, 0, 0))],
        out_specs=[blk,
                   pl.BlockSpec((None, None, ATTN_BLOCK, N_HEADS_A), lambda bi, r, n: (bi, r, n, 0))],
        out_shape=[jax.ShapeDtypeStruct((b, length, dil * D_A), BF16),
                   jax.ShapeDtypeStruct((b, dil, length, N_HEADS_A), F32)],
        scratch_shapes=[pltpu.VMEM((ATTN_BLOCK, D_A), BF16), pltpu.VMEM((ATTN_BLOCK, D_A), BF16)],
        compiler_params=_cparams(("arbitrary", "arbitrary", "arbitrary"), 32),
        name=f"attn_prompt_d{dil}",
    )(view(q), view(k), view(v), bias)
    return o.reshape(b * t, D_A), lse


def _lower_bound(lbp_ref):
    p0 = lbp_ref[0:1, :]
    p1 = lbp_ref[1:2, :]
    m = jnp.maximum(p0, p1)
    e0 = jnp.exp(p0 - m)
    e1 = jnp.exp(p1 - m)
    return e0 / (e0 + e1)


def _gates(qb, fb, lb):
    f = lb + (1.0 - lb) * jax.nn.sigmoid(fb)
    return _silu(qb) * (EXPAND_B ** -0.5), 1.0 - f, jnp.log(f)


def _iota2(shape, dim):
    return lax.broadcasted_iota(jnp.int32, shape, dim)


def _div2(x, n):
    return jnp.right_shift(x, int(n).bit_length() - 1)


def _mod2(x, n):
    return jnp.bitwise_and(x, n - 1)


def _head_expand_matrix():
    return (_iota2((LANES, D_A), 0) == _div2(_iota2((LANES, D_A), 1), HEAD_DIM_A)).astype(BF16)


def _diag_block_scores(q, kk, b, a_heads, row, col, block):
    same = _div2(row, block) == _div2(col, block)
    for j in range(block):
        bj = _bcast_rows(b, block, j)
        kj = _bcast_rows(kk, block, j)
        x = q * jnp.exp(jnp.minimum(b - bj, 0.0)) * kj
        sel = same & (_mod2(col, block) == j) & (_mod2(row, block) >= j)
        for h in range(N_HEADS_B):
            cs = jnp.sum(x[:, h * EXPAND_B:(h + 1) * EXPAND_B], axis=-1, keepdims=True)
            a_heads[h] = jnp.where(sel, cs, a_heads[h])
    return a_heads


def _hgrn_prompt_body(qb_ref, fb_ref, ib_ref, lbp_ref, o_ref, s_ref, st_ref, *, n_chunks):
    t = pl.program_id(1)
    c_len = GLA_CHUNK

    @pl.when(t == 0)
    def _():
        st_ref[...] = jnp.zeros_like(st_ref)

    lb = _lower_bound(lbp_ref)
    row = _iota2((c_len, c_len), 0)
    col = _iota2((c_len, c_len), 1)
    tri = (col <= row).astype(BF16)
    levels = []
    for n in (32, 16, 8):
        mask = ((_div2(row, 2 * n) == _div2(col, 2 * n)) & (_mod2(_div2(row, n), 2) == 1)
                & (_mod2(_div2(col, n), 2) == 0))
        levels.append((n, mask))

    def chunk(c, carry):
        r0 = pl.multiple_of(c * c_len, c_len)
        rows = pl.ds(r0, c_len)
        q, kk, g = _gates(qb_ref[rows, :], fb_ref[rows, :], lb)
        v = ib_ref[rows, :]
        b = _dot3(tri, g)
        b_end = b[c_len - 1:c_len, :]
        q_in = (q * jnp.exp(b)).astype(BF16)
        k_out = (kk * jnp.exp(b_end - b)).astype(BF16)
        v16 = v.astype(BF16)
        dec = jnp.exp(b_end)

        a_heads = [jnp.zeros((c_len, c_len), F32) for _ in range(N_HEADS_B)]
        for n, mask in levels:
            beta = _bcast_rows(b, 2 * n, n - 1)
            ql = (q * jnp.exp(jnp.minimum(b - beta, 0.0))).astype(BF16)
            kl = (kk * jnp.exp(jnp.minimum(beta - b, 0.0))).astype(BF16)
            for h in range(N_HEADS_B):
                hs = slice(h * EXPAND_B, (h + 1) * EXPAND_B)
                part = lax.dot_general(ql[:, hs], kl[:, hs], NT_DIMS, preferred_element_type=F32)
                a_heads[h] = a_heads[h] + jnp.where(mask, part, 0.0)
        a_heads = _diag_block_scores(q, kk, b, a_heads, row, col, 8)

        for h in range(N_HEADS_B):
            hs = slice(h * EXPAND_B, (h + 1) * EXPAND_B)
            st = st_ref[h]
            inter = lax.dot_general(q_in[:, hs], st.astype(BF16), NT_DIMS, preferred_element_type=F32)
            intra = _dot(a_heads[h].astype(BF16), v16[:, hs])
            o_ref[rows, hs] = inter + intra
            upd = lax.dot_general(v16[:, hs], k_out[:, hs], TN_DIMS, preferred_element_type=F32)
            st_ref[h] = st * dec[:, hs] + upd
        return carry

    lax.fori_loop(0, n_chunks, chunk, 0)

    @pl.when(t == pl.num_programs(1) - 1)
    def _():
        for h in range(N_HEADS_B):
            s_ref[h] = st_ref[h].T


def _hgrn_prompt(qb, fb, ib, lb_param, b, t, tt):
    nt = t // tt
    blk = pl.BlockSpec((tt, D_F), lambda bi, ti: (bi * nt + ti, 0))
    return pl.pallas_call(
        functools.partial(_hgrn_prompt_body, n_chunks=tt // GLA_CHUNK),
        grid=(b, nt),
        in_specs=[blk, blk, blk, pl.BlockSpec((DEPTH + 1, D_F), lambda bi, ti: (0, 0))],
        out_specs=[blk,
                   pl.BlockSpec((None, N_HEADS_B, EXPAND_B, HEAD_V_B), lambda bi, ti: (bi, 0, 0, 0))],
        out_shape=[jax.ShapeDtypeStruct((b * t, D_B), F32),
                   jax.ShapeDtypeStruct((b, N_HEADS_B, EXPAND_B, HEAD_V_B), F32)],
        scratch_shapes=[pltpu.VMEM((N_HEADS_B, HEAD_V_B, EXPAND_B), F32)],
        compiler_params=_cparams(("arbitrary", "arbitrary"), 32),
        name="hgrn_prompt",
    )(qb, fb, ib, lb_param)


def _hgrn_sample_body(qb_ref, fb_ref, ib_ref, lbp_ref, s0_ref, o_ref, s_ref, *, nbatch):
    rows_n = nbatch * DEC_SEQ
    assert rows_n == LANES
    lb = _lower_bound(lbp_ref)
    row = _iota2((rows_n, rows_n), 0)
    col = _iota2((rows_n, rows_n), 1)
    same = _div2(row, DEC_SEQ) == _div2(col, DEC_SEQ)
    tri = (same & (col <= row)).astype(BF16)
    pick = (col == row * DEC_SEQ + (DEC_SEQ - 1)).astype(BF16)

    q, kk, g = _gates(qb_ref[...], fb_ref[...], lb)
    v = ib_ref[...]
    b = _dot3(tri, g)
    b_end = _bcast_rows(b, DEC_SEQ, DEC_SEQ - 1)
    q_in = (q * jnp.exp(b)).astype(BF16)
    k_out = kk * jnp.exp(b_end - b)
    v16 = v.astype(BF16)

    a_heads = [jnp.zeros((rows_n, rows_n), F32) for _ in range(N_HEADS_B)]
    a_heads = _diag_block_scores(q, kk, b, a_heads, row, col, DEC_SEQ)

    rsel = _div2(_iota2((rows_n, 1), 0), DEC_SEQ)
    for h in range(N_HEADS_B):
        hs = slice(h * EXPAND_B, (h + 1) * EXPAND_B)
        intra = _dot(a_heads[h].astype(BF16), v16[:, hs])
        dec_t = jnp.exp(_dot3(pick, b[:, hs])).T
        k_out_t = k_out[:, hs].T.astype(BF16)
        for bi in range(nbatch):
            rs = slice(bi * DEC_SEQ, (bi + 1) * DEC_SEQ)
            s0 = s0_ref[bi, h]
            inter = _dot(q_in[rs, hs], s0.astype(BF16))
            o_ref[rs, hs] = inter + intra[rs, :]
            v_b = jnp.where(rsel == bi, v16[:, hs], jnp.zeros_like(v16[:, hs]))
            upd = _dot(k_out_t, v_b)
            s_ref[bi, h] = s0 * dec_t[:, bi:bi + 1] + upd


def _hgrn_sample(qb, fb, ib, lb_param, s0, nbatch):
    b = s0.shape[0]
    rows_n = nbatch * DEC_SEQ
    blk = pl.BlockSpec((rows_n, D_F), lambda i: (i, 0))
    sblk = pl.BlockSpec((nbatch, N_HEADS_B, EXPAND_B, HEAD_V_B), lambda i: (i, 0, 0, 0))
    return pl.pallas_call(
        functools.partial(_hgrn_sample_body, nbatch=nbatch),
        grid=(b // nbatch,),
        in_specs=[blk, blk, blk, pl.BlockSpec((DEPTH + 1, D_F), lambda i: (0, 0)), sblk],
        out_specs=[blk, sblk],
        out_shape=[jax.ShapeDtypeStruct((b * DEC_SEQ, D_B), F32),
                   jax.ShapeDtypeStruct(s0.shape, F32)],
        compiler_params=_cparams(("arbitrary",), 32),
        name="hgrn_sample",
    )(qb, fb, ib, lb_param, s0)


EXT_KEYS = WINDOW_MAX + LANES
SHIFT_ROWS = 64


def _sample_key_tables():
    t = np.arange(DEC_SEQ)[:, None]
    e = np.arange(EXT_KEYS)[None, :]
    is_new = e >= WINDOW_MAX
    d = np.where(is_new, t - (e - WINDOW_MAX), WINDOW_MAX + t - e)
    masks = np.full((len(DILATIONS), DEC_SEQ, EXT_KEYS), -np.inf, np.float32)
    for ci, dil in enumerate(DILATIONS):
        ok = (d >= 0) & (d % dil == 0) & (d <= WIN_STEPS * dil) & (e < WINDOW_MAX + DEC_SEQ)
        masks[ci][ok] = 0.0
    return np.maximum(d, 0), masks


def _attn_sample_body(q_ref, kn_ref, vn_ref, ck_ref, cv_ref, base_ref, mask_ref, attn_ref, ok_ref, ov_ref):
    keep = LANES - DEC_SEQ
    lane = _iota2((SHIFT_ROWS, LANES), 1)
    pad_rows = jnp.zeros((keep, D_A), F32)
    for c_ref, n_ref, o_ref in ((ck_ref, kn_ref, ok_ref), (cv_ref, vn_ref, ov_ref)):
        new_t = jnp.concatenate([pad_rows, n_ref[...]], axis=0).T
        for r0 in range(0, D_A, SHIFT_ROWS):
            rows = slice(r0, r0 + SHIFT_ROWS)
            rolled = pltpu.roll(c_ref[rows, :], WINDOW_MAX - DEC_SEQ, axis=1)
            o_ref[rows, 0:WINDOW_MAX - LANES] = rolled[:, 0:WINDOW_MAX - LANES]
            o_ref[rows, WINDOW_MAX - LANES:] = jnp.where(lane >= keep, new_t[rows, :], rolled[:, WINDOW_MAX - LANES:])

    qs = q_ref[...] * ATTN_SCALE
    pad_new = jnp.zeros((keep, HEAD_DIM_A), F32)
    for h in range(N_HEADS_A):
        hs = slice(HEAD_DIM_A * h, HEAD_DIM_A * (h + 1))
        qh = qs[:, hs].astype(BF16)
        s_cache = _dot(qh, ck_ref[hs, :].astype(BF16))
        kn_pad = jnp.concatenate([kn_ref[:, hs], pad_new], axis=0).astype(BF16)
        s_new = lax.dot_general(qh, kn_pad, NT_DIMS, preferred_element_type=F32)
        s = jnp.concatenate([s_cache, s_new], axis=1) + base_ref[h]
        probs, lses = [], []
        for ci in range(len(DILATIONS)):
            sc = s + mask_ref[ci]
            m = jnp.max(sc, axis=1, keepdims=True)
            p = jnp.exp(sc - m)
            den = jnp.sum(p, axis=1, keepdims=True)
            probs.append(p / den)
            lses.append(m + jnp.log(den))
        lmax = jnp.maximum(jnp.maximum(lses[0], lses[1]), lses[2])
        ws = [jnp.exp(l - lmax) for l in lses]
        wsum = ws[0] + ws[1] + ws[2]
        pw = (probs[0] * (ws[0] / wsum) + probs[1] * (ws[1] / wsum) + probs[2] * (ws[2] / wsum)).astype(BF16)
        o = lax.dot_general(pw[:, 0:WINDOW_MAX], cv_ref[hs, :].astype(BF16), NT_DIMS, preferred_element_type=F32)
        vn_pad = jnp.concatenate([vn_ref[:, hs], pad_new], axis=0).astype(BF16)
        attn_ref[:, hs] = o + _dot(pw[:, WINDOW_MAX:], vn_pad)


def _attn_sample(q, kn, vn, cache_kt, cache_vt, base, masks):
    b = cache_kt.shape[0]
    small = pl.BlockSpec((DEC_SEQ, D_A), lambda i: (i, 0))
    big = pl.BlockSpec((None, D_A, WINDOW_MAX), lambda i: (i, 0, 0))
    const3 = lambda a: pl.BlockSpec(a.shape, lambda i: (0, 0, 0))
    return pl.pallas_call(
        _attn_sample_body,
        grid=(b,),
        in_specs=[small, small, small, big, big, const3(base), const3(masks)],
        out_specs=[small, big, big],
        out_shape=[jax.ShapeDtypeStruct((b * DEC_SEQ, D_A), F32),
                   jax.ShapeDtypeStruct(cache_kt.shape, F32),
                   jax.ShapeDtypeStruct(cache_vt.shape, F32)],
        compiler_params=_cparams(("arbitrary",), 56),
        name="attn_sample",
    )(q, kn, vn, cache_kt, cache_vt, base, masks)


def _merge_body(*refs, n_cfg):
    x_ref = refs[0]
    o_refs = refs[1:1 + n_cfg]
    l_refs = refs[1 + n_cfg:1 + 2 * n_cfg] if n_cfg > 1 else ()
    ga_ref, ob_ref, gb_ref, wout_ref, ng_ref, lg_ref, lbias_ref, y_ref = refs[1 + n_cfg + len(l_refs):]

    if n_cfg > 1:
        et_mat = _head_expand_matrix()
        ls =[r[...] for r in l_refs]
        lmax = functools.reduce(jnp.maximum, ls)
        ws = [jnp.exp(l - lmax) for l in ls]
        wsum = functools.reduce(lambda a, c: a + c, ws)
        attn = None
        for w, o_ref in zip(ws, o_refs):
            term = _dot2_rhs(w / wsum, et_mat) * o_ref[...].astype(F32)
            attn = term if attn is None else attn + term
    else:
        attn = o_refs[0][...].astype(F32)

    a_out = attn * _silu(ga_ref[...])
    parts = [a_out.astype(BF16)]
    for h in range(N_HEADS_B):
        hs = slice(h * HEAD_V_B, (h + 1) * HEAD_V_B)
        oh = ob_ref[:, hs]
        ms = jnp.mean(oh * oh, axis=-1, keepdims=True)
        on = oh * lax.rsqrt(ms + NORM_EPS) * ng_ref[...]
        parts.append((on * _silu(gb_ref[:, hs])).astype(BF16))
    mix = jnp.concatenate(parts, axis=-1)
    z = ALPHA * x_ref[...] + _dot(mix, wout_ref[...])
    mu = jnp.mean(z, axis=-1, keepdims=True)
    zc = z - mu
    var = jnp.mean(zc * zc, axis=-1, keepdims=True)
    y_ref[...] = zc * lax.rsqrt(var + NORM_EPS) * lg_ref[...] + lbias_ref[...]


def _merge(x2d, attn_outs, lses, ga, ob, gb, w_out_bf16, norm_g, ln_g, ln_b, tm):
    n = x2d.shape[0]
    n_cfg = len(attn_outs)
    row = lambda width: pl.BlockSpec((tm, width), lambda i: (i, 0))
    const = lambda shape: pl.BlockSpec(shape, lambda i: (0,) * len(shape))
    in_specs = ([row(D_MODEL)] + [row(D_A)] * n_cfg + [row(LANES)] * len(lses)
                + [row(D_A), row(D_B), row(D_B), const((D_MODEL, D_MODEL)),
                   const((1, HEAD_V_B)), const((1, D_MODEL)), const((1, D_MODEL))])
    return pl.pallas_call(
        functools.partial(_merge_body, n_cfg=n_cfg),
        grid=(n // tm,),
        in_specs=in_specs,
        out_specs=row(D_MODEL),
        out_shape=jax.ShapeDtypeStruct((n, D_MODEL), F32),
        compiler_params=_cparams(("arbitrary",), 48),
        name=f"merge_{n_cfg}",
    )(x2d, *attn_outs, *lses, ga, ob, gb, w_out_bf16, norm_g, ln_g, ln_b)


def _bias_by_distance(rel_bias, dist):
    onehot = jax.nn.one_hot(_rel_buckets(dist.reshape(-1)), NUM_BUCKETS, dtype=F32)
    table = jnp.dot(onehot, rel_bias.astype(F32), precision=lax.Precision.HIGHEST)
    return table.reshape(dist.shape + (N_HEADS_A,))


def _prompt_bias(rel_bias, dil):
    i = np.arange(ATTN_BLOCK)[:, None]
    j = np.arange(2 * ATTN_BLOCK)[None, :]
    sub = ATTN_BLOCK + i - j
    ok = (sub >= 0) & (sub <= WIN_STEPS)
    bias = _bias_by_distance(rel_bias, dil * np.clip(sub, 0, WIN_STEPS))
    bias = jnp.where(ok[:, :, None], bias, -jnp.inf)
    return jnp.transpose(bias, (2, 0, 1))


def _to_feature_major(cache):
    b, p, h, d = cache.shape
    return jnp.transpose(cache, (0, 2, 3, 1)).reshape(b, h * d, p)


def _from_feature_major(cache_t):
    b, _, p = cache_t.shape
    return jnp.transpose(cache_t.reshape(b, N_HEADS_A, HEAD_DIM_A, p), (0, 3, 1, 2))


def kernel(x_prompt, x_sample, cache_k, cache_v, state_hgrn, w_in, w_out, rel_bias, lb_param, hgrn_norm_g, ln_g, ln_b):
    bsz, seq, _ = x_prompt.shape
    dec_b, dec_t, _ = x_sample.shape
    assert dec_t == DEC_SEQ and cache_k.shape[2] == WINDOW_MAX and w_in.shape[0] == DEPTH
    keep = min(WINDOW_MAX, seq)

    w_in16 = w_in[0].astype(BF16)
    w_out16 = w_out[0].astype(BF16)
    norm_g = hgrn_norm_g[0].reshape(1, HEAD_V_B).astype(F32)
    lng = ln_g[0].reshape(1, D_MODEL).astype(F32)
    lnb = ln_b[0].reshape(1, D_MODEL).astype(F32)
    lbp = lb_param.astype(F32)

    tm = 512
    n_p = bsz * seq
    tiles_per_seq = seq // tm
    win_tiles = keep // tm
    main = lambda i: (i, 0)
    win = lambda i: ((i // tiles_per_seq) * win_tiles + jnp.maximum(i % tiles_per_seq - (tiles_per_seq - win_tiles), 0), 0)
    o_qa, o_ka, o_va, o_ga, o_qb, o_fb, o_ib, o_gb = (D_A * 0, D_A * 1, D_A * 2, D_A * 3, 4 * D_A,
                                                    4 * D_A + D_F, 4 * D_A + 2 * D_F, 4 * D_A + 2 * D_F + D_B)
    outs = [
        (o_qa, D_A, ATTN_SCALE, BF16, n_p, main),
        (o_ka, D_A, 1.0, BF16, n_p, main),
        (o_va, D_A, 1.0, BF16, n_p, main),
        (o_ka, D_A, 1.0, F32, bsz * keep, win),
        (o_va, D_A, 1.0, F32, bsz * keep, win),
        (o_ga, D_A, 1.0, F32, n_p, main),
        (o_qb, D_F, 1.0, F32, n_p, main),
        (o_fb, D_F, 1.0, F32, n_p, main),
        (o_ib, D_B, 1.0, F32, n_p, main),
        (o_gb, D_B, 1.0, F32, n_p, main),
    ]
    x2d = x_prompt.reshape(n_p, D_MODEL)
    qa, ka, va, kwin, vwin, ga, qb, fb, ib, gb = _proj(x2d, w_in16, outs, tm)

    attn_outs, lses = [], []
    for dil in DILATIONS:
        o, lse = _attn_prompt(qa.reshape(bsz, seq, D_A), ka.reshape(bsz, seq, D_A), va.reshape(bsz, seq, D_A),
                              _prompt_bias(rel_bias, dil), dil)
        lse = jnp.transpose(lse, (0, 2, 1, 3)).reshape(n_p, N_HEADS_A)
        attn_outs.append(o)
        lses.append(jnp.pad(lse, ((0, 0), (0, LANES - N_HEADS_A))))
    ob, s_prompt = _hgrn_prompt(qb, fb, ib, lbp, bsz, seq, 512)
    y_prompt = _merge(x2d, attn_outs, lses, ga, ob, gb, w_out16, norm_g, lng, lnb, tm).reshape(bsz, seq, D_MODEL)

    n_s = dec_b * dec_t
    outs_s = [(lo, D_A, 1.0, F32, n_s, main) for lo in (o_qa, o_ka, o_va, o_ga, o_qb, o_fb, o_ib, o_gb)]
    xs2d = x_sample.reshape(n_s, D_MODEL)
    tm_s = min(tm, n_s)
    sqa, ska, sva, sga, sqb, sfb, sib, sgb = _proj(xs2d, w_in16, outs_s, tm_s)
    dist_s, masks_s = _sample_key_tables()
    base_s = jnp.transpose(_bias_by_distance(rel_bias, dist_s), (2, 0, 1))
    attn_s, k_new_t, v_new_t = _attn_sample(sqa, ska, sva, _to_feature_major(cache_k[0]), _to_feature_major(cache_v[0]),
                                            base_s, jnp.asarray(masks_s))
    ob_s, s_sample = _hgrn_sample(sqb, sfb, sib, lbp, state_hgrn[0], LANES // DEC_SEQ)
    y_sample = _merge(xs2d, [attn_s], [], sga, ob_s, sgb, w_out16, norm_g, lng, lnb, tm_s).reshape(dec_b, dec_t, D_MODEL)

    win_shape = (DEPTH, bsz, keep, N_HEADS_A, HEAD_DIM_A)
    return (y_prompt, y_sample,
            kwin.reshape(win_shape), vwin.reshape(win_shape),
            _from_feature_major(k_new_t)[None], _from_feature_major(v_new_t)[None],
            s_prompt[None], s_sample[None].astype(state_hgrn.dtype))
```

```python
import functools

import numpy as np
import jax
import jax.numpy as jnp
from jax import lax
from jax.experimental import pallas as pl
from jax.experimental.pallas import tpu as pltpu

F32 = jnp.float32
BF16 = jnp.bfloat16

D_MODEL = 1024
D_A = 512
HEAD_DIM_A = 64
N_HEADS_A = 8
DILATIONS = (1, 4, 16)
WIN_STEPS = 128
WINDOW_MAX = 2048
ATTN_BLOCK = 128
ATTN_SCALE = HEAD_DIM_A ** -0.5
NUM_BUCKETS = 32
REL_MAX_DIST = 2048
D_B = 512
N_HEADS_B = 4
HEAD_V_B = 128
EXPAND_B = 128
D_F = 512
GLA_CHUNK = 64
D_IN = 4 * D_A + 2 * D_F + 2 * D_B
DEPTH = 1
ALPHA = (2.0 * DEPTH) ** 0.25
NORM_EPS = 1e-5
DEC_SEQ = 8

LANES = 128
MIB = 1024 * 1024

NT_DIMS = (((1,), (1,)), ((), ()))
TN_DIMS = (((0,), (0,)), ((), ()))


def _cparams(semantics, vmem_mib):
    return pltpu.CompilerParams(dimension_semantics=semantics, vmem_limit_bytes=vmem_mib * MIB)


def _rel_buckets(dist):
    max_exact = NUM_BUCKETS // 2
    d = np.maximum(dist, 1).astype(np.float32)
    large = max_exact + (np.log(d / max_exact) / np.log(REL_MAX_DIST / max_exact)
                         * (NUM_BUCKETS - max_exact)).astype(np.int32)
    large = np.minimum(large, NUM_BUCKETS - 1)
    return np.where(dist < max_exact, dist, large).astype(np.int32)


def _split3(x):
    hi = x.astype(BF16)
    r1 = x - hi.astype(F32)
    mid = r1.astype(BF16)
    lo = (r1 - mid.astype(F32)).astype(BF16)
    return hi, mid, lo


def _dot(a, b):
    return jnp.dot(a, b, preferred_element_type=F32)


def _dot3(mat_bf16, x):
    hi, mid, lo = _split3(x)
    return _dot(mat_bf16, hi) + _dot(mat_bf16, mid) + _dot(mat_bf16, lo)


def _dot2_rhs(x, mat_bf16):
    hi = x.astype(BF16)
    lo = (x - hi.astype(F32)).astype(BF16)
    return _dot(hi, mat_bf16) + _dot(lo, mat_bf16)


def _silu(x):
    return x * jax.nn.sigmoid(x)


def _bcast_rows(x, block, j):
    rows = x.shape[0]
    parts = [jnp.broadcast_to(x[r0 + j:r0 + j + 1, :], (block, x.shape[1])) for r0 in range(0, rows, block)]
    return parts[0] if len(parts) == 1 else jnp.concatenate(parts, axis=0)


def _proj_body(x_ref, w_ref, *out_refs, groups):
    x = x_ref[...].astype(BF16)
    for (lo, width, scale, idxs) in groups:
        h = _dot(x, w_ref[:, lo:lo + width])
        if scale != 1.0:
            h = h * scale
        for i in idxs:
            out_refs[i][...] = h.astype(out_refs[i].dtype)


def _proj(x2d, w_bf16, outs, tm):
    n = x2d.shape[0]
    groups = {}
    for i, (lo, width, scale, _, _, _) in enumerate(outs):
        groups.setdefault((lo, width, scale), []).append(i)
    groups = tuple((lo, width, scale, tuple(idxs)) for (lo, width, scale), idxs in groups.items())
    out_shape = [jax.ShapeDtypeStruct((rows, width), dt) for (_, width, _, dt, rows, _) in outs]
    out_specs = [pl.BlockSpec((tm, width), imap) for (_, width, _, _, _, imap) in outs]
    return pl.pallas_call(
        functools.partial(_proj_body, groups=groups),
        grid=(n // tm,),
        in_specs=[pl.BlockSpec((tm, D_MODEL), lambda i: (i, 0)),
                  pl.BlockSpec((D_MODEL, D_IN), lambda i: (0, 0))],
        out_specs=out_specs,
        out_shape=out_shape,
        compiler_params=_cparams(("arbitrary",), 56),
        name="proj",
    )(x2d, w_bf16)


ATT_TILE = WIN_STEPS * max(DILATIONS)
N_PAIRS = D_A // LANES


def _rows(start, dil):
    return pl.ds(start, ATTN_BLOCK) if dil == 1 else pl.ds(start, ATTN_BLOCK, stride=dil)


def _attn_prompt_body(q_ref, k_ref, v_ref, bias_ref, o_ref, lse_ref, qs, kring, vring, os_, ls, *, dil):
    t = pl.program_id(1)
    slot = lax.rem(t, 2)
    cur_base = pl.multiple_of(slot * ATT_TILE, ATT_TILE)
    other_base = pl.multiple_of((1 - slot) * ATT_TILE, ATT_TILE)
    for j in range(N_PAIRS):
        lanes = slice(LANES * j, LANES * (j + 1))
        qs[j] = q_ref[:, lanes].astype(F32)
        kring[j, pl.ds(cur_base, ATT_TILE), :] = k_ref[:, lanes].astype(F32)
        vring[j, pl.ds(cur_base, ATT_TILE), :] = v_ref[:, lanes].astype(F32)

    @pl.when(t == 0)
    def _():
        zeros = jnp.zeros((ATT_TILE, LANES), F32)
        for j in range(N_PAIRS):
            kring[j, pl.ds(other_base, ATT_TILE), :] = zeros
            vring[j, pl.ds(other_base, ATT_TILE), :] = zeros

    lane = _iota2((ATTN_BLOCK, LANES), 1)
    low_half = lane < HEAD_DIM_A
    prev_cols = _iota2((ATTN_BLOCK, 2 * ATTN_BLOCK), 1) < ATTN_BLOCK
    span = dil * ATTN_BLOCK

    def unit(u, carry):
        r = lax.rem(u, dil)
        n = lax.div(u, dil)
        off = r + span * n
        prev_start = jnp.where(n >= 1, cur_base + off - span, other_base + ATT_TILE - span + r)
        no_prev = jnp.logical_and(t == 0, n == 0)
        q_rows = _rows(off, dil)
        cur_rows = _rows(cur_base + off, dil)
        prev_rows = _rows(prev_start, dil)

        scores = []
        for j in range(N_PAIRS):
            qj = qs[j, q_rows, :]
            kcat = jnp.concatenate([kring[j, prev_rows, :], kring[j, cur_rows, :]], axis=0).astype(BF16)
            for qh in (jnp.where(low_half, qj, 0.0), jnp.where(low_half, 0.0, qj)):
                scores.append(lax.dot_general(qh.astype(BF16), kcat, NT_DIMS, preferred_element_type=F32))

        probs, inv_dens = [], []
        lse_tile = jnp.zeros((ATTN_BLOCK, LANES), F32)
        for h in range(N_HEADS_A):
            s = scores[h] + bias_ref[h]
            s = jnp.where(jnp.logical_and(no_prev, prev_cols), -jnp.inf, s)
            m = jnp.max(jnp.maximum(s[:, :ATTN_BLOCK], s[:, ATTN_BLOCK:]), axis=1, keepdims=True)
            p = jnp.exp(s - m)
            den = jnp.sum(p[:, :ATTN_BLOCK] + p[:, ATTN_BLOCK:], axis=1, keepdims=True)
            probs.append(p.astype(BF16))
            inv_dens.append(1.0 / den)
            lse_tile = jnp.where(lane == h, m + jnp.log(den), lse_tile)

        for j in range(N_PAIRS):
            vcat = jnp.concatenate([vring[j, prev_rows, :], vring[j, cur_rows, :]], axis=0).astype(BF16)
            o_a = _dot(probs[2 * j], vcat) * inv_dens[2 * j]
            o_b = _dot(probs[2 * j + 1], vcat) * inv_dens[2 * j + 1]
            os_[j, q_rows, :] = jnp.where(low_half, o_a, o_b)
        ls[q_rows, :] = lse_tile
        return carry

    lax.fori_loop(0, ATT_TILE // ATTN_BLOCK, unit, 0)

    for j in range(N_PAIRS):
        o_ref[:, LANES * j:LANES * (j + 1)] = os_[j].astype(o_ref.dtype)
    lse_ref[...] = ls[...]


def _attn_prompt(q, k, v, bias, dil, b, t):
    nt = t // ATT_TILE
    blk = pl.BlockSpec((ATT_TILE, D_A), lambda bi, ti: (bi * nt + ti, 0))
    lse_blk = pl.BlockSpec((ATT_TILE, LANES), lambda bi, ti: (bi * nt + ti, 0))
    slab = lambda rows: pltpu.VMEM((N_PAIRS, rows, LANES), F32)
    return pl.pallas_call(
        functools.partial(_attn_prompt_body, dil=dil),
        grid=(b, nt),
        in_specs=[blk, blk, blk,
                  pl.BlockSpec((N_HEADS_A, ATTN_BLOCK, 2 * ATTN_BLOCK), lambda bi, ti: (0, 0, 0))],
        out_specs=[blk, lse_blk],
        out_shape=[jax.ShapeDtypeStruct((b * t, D_A), BF16),
                   jax.ShapeDtypeStruct((b * t, LANES), F32)],
        scratch_shapes=[slab(ATT_TILE), slab(2 * ATT_TILE), slab(2 * ATT_TILE), slab(ATT_TILE),
                        pltpu.VMEM((ATT_TILE, LANES), F32)],
        compiler_params=_cparams(("arbitrary", "arbitrary"), 56),
        name=f"attn_prompt_d{dil}",
    )(q, k, v, bias)


def _lower_bound(lbp_ref):
    p0 = lbp_ref[0:1, :]
    p1 = lbp_ref[1:2, :]
    m = jnp.maximum(p0, p1)
    e0 = jnp.exp(p0 - m)
    e1 = jnp.exp(p1 - m)
    return e0 / (e0 + e1)


def _gates(qb, fb, lb):
    f = lb + (1.0 - lb) * jax.nn.sigmoid(fb)
    return _silu(qb) * (EXPAND_B ** -0.5), 1.0 - f, jnp.log(f)


def _iota2(shape, dim):
    return lax.broadcasted_iota(jnp.int32, shape, dim)


def _div2(x, n):
    return jnp.right_shift(x, int(n).bit_length() - 1)


def _mod2(x, n):
    return jnp.bitwise_and(x, n - 1)


def _head_expand_matrix():
    return (_iota2((LANES, D_A), 0) == _div2(_iota2((LANES, D_A), 1), HEAD_DIM_A)).astype(BF16)


def _diag_block_scores(q, kk, b, a_heads, row, col, block):
    same = _div2(row, block) == _div2(col, block)
    for j in range(block):
        bj = _bcast_rows(b, block, j)
        kj = _bcast_rows(kk, block, j)
        x = q * jnp.exp(jnp.minimum(b - bj, 0.0)) * kj
        sel = same & (_mod2(col, block) == j) & (_mod2(row, block) >= j)
        for h in range(N_HEADS_B):
            cs = jnp.sum(x[:, h * EXPAND_B:(h + 1) * EXPAND_B], axis=-1, keepdims=True)
            a_heads[h] = jnp.where(sel, cs, a_heads[h])
    return a_heads


def _hgrn_prompt_body(qb_ref, fb_ref, ib_ref, lbp_ref, o_ref, s_ref, st_ref, *, n_chunks):
    t = pl.program_id(1)
    c_len = GLA_CHUNK

    @pl.when(t == 0)
    def _():
        st_ref[...] = jnp.zeros_like(st_ref)

    lb = _lower_bound(lbp_ref)
    row = _iota2((c_len, c_len), 0)
    col = _iota2((c_len, c_len), 1)
    tri = (col <= row).astype(BF16)
    levels = []
    for n in (32, 16, 8):
        mask = ((_div2(row, 2 * n) == _div2(col, 2 * n)) & (_mod2(_div2(row, n), 2) == 1)
                & (_mod2(_div2(col, n), 2) == 0))
        levels.append((n, mask))

    def chunk(c, carry):
        r0 = pl.multiple_of(c * c_len, c_len)
        rows = pl.ds(r0, c_len)
        q, kk, g = _gates(qb_ref[rows, :], fb_ref[rows, :], lb)
        v = ib_ref[rows, :]
        b = _dot3(tri, g)
        b_end = b[c_len - 1:c_len, :]
        q_in = (q * jnp.exp(b)).astype(BF16)
        k_out = (kk * jnp.exp(b_end - b)).astype(BF16)
        v16 = v.astype(BF16)
        dec = jnp.exp(b_end)

        a_heads = [jnp.zeros((c_len, c_len), F32) for _ in range(N_HEADS_B)]
        for n, mask in levels:
            beta = _bcast_rows(b, 2 * n, n - 1)
            ql = (q * jnp.exp(jnp.minimum(b - beta, 0.0))).astype(BF16)
            kl = (kk * jnp.exp(jnp.minimum(beta - b, 0.0))).astype(BF16)
            for h in range(N_HEADS_B):
                hs = slice(h * EXPAND_B, (h + 1) * EXPAND_B)
                part = lax.dot_general(ql[:, hs], kl[:, hs], NT_DIMS, preferred_element_type=F32)
                a_heads[h] = a_heads[h] + jnp.where(mask, part, 0.0)
        a_heads = _diag_block_scores(q, kk, b, a_heads, row, col, 8)

        for h in range(N_HEADS_B):
            hs = slice(h * EXPAND_B, (h + 1) * EXPAND_B)
            st = st_ref[h]
            inter = lax.dot_general(q_in[:, hs], st.astype(BF16), NT_DIMS, preferred_element_type=F32)
            intra = _dot(a_heads[h].astype(BF16), v16[:, hs])
            o_ref[rows, hs] = inter + intra
            upd = lax.dot_general(v16[:, hs], k_out[:, hs], TN_DIMS, preferred_element_type=F32)
            st_ref[h] = st * dec[:, hs] + upd
        return carry

    lax.fori_loop(0, n_chunks, chunk, 0)

    @pl.when(t == pl.num_programs(1) - 1)
    def _():
        for h in range(N_HEADS_B):
            s_ref[h] = st_ref[h].T


def _hgrn_prompt(qb, fb, ib, lb_param, b, t, tt):
    nt = t // tt
    blk = pl.BlockSpec((tt, D_F), lambda bi, ti: (bi * nt + ti, 0))
    return pl.pallas_call(
        functools.partial(_hgrn_prompt_body, n_chunks=tt // GLA_CHUNK),
        grid=(b, nt),
        in_specs=[blk, blk, blk, pl.BlockSpec((DEPTH + 1, D_F), lambda bi, ti: (0, 0))],
        out_specs=[blk,
                   pl.BlockSpec((None, N_HEADS_B, EXPAND_B, HEAD_V_B), lambda bi, ti: (bi, 0, 0, 0))],
        out_shape=[jax.ShapeDtypeStruct((b * t, D_B), F32),
                   jax.ShapeDtypeStruct((b, N_HEADS_B, EXPAND_B, HEAD_V_B), F32)],
        scratch_shapes=[pltpu.VMEM((N_HEADS_B, HEAD_V_B, EXPAND_B), F32)],
        compiler_params=_cparams(("arbitrary", "arbitrary"), 32),
        name="hgrn_prompt",
    )(qb, fb, ib, lb_param)


def _hgrn_sample_body(qb_ref, fb_ref, ib_ref, lbp_ref, s0_ref, o_ref, s_ref, *, nbatch):
    rows_n = nbatch * DEC_SEQ
    assert rows_n == LANES
    lb = _lower_bound(lbp_ref)
    row = _iota2((rows_n, rows_n), 0)
    col = _iota2((rows_n, rows_n), 1)
    same = _div2(row, DEC_SEQ) == _div2(col, DEC_SEQ)
    tri = (same & (col <= row)).astype(BF16)
    pick = (col == row * DEC_SEQ + (DEC_SEQ - 1)).astype(BF16)

    q, kk, g = _gates(qb_ref[...], fb_ref[...], lb)
    v = ib_ref[...]
    b = _dot3(tri, g)
    b_end = _bcast_rows(b, DEC_SEQ, DEC_SEQ - 1)
    q_in = (q * jnp.exp(b)).astype(BF16)
    k_out = kk * jnp.exp(b_end - b)
    v16 = v.astype(BF16)

    a_heads = [jnp.zeros((rows_n, rows_n), F32) for _ in range(N_HEADS_B)]
    a_heads = _diag_block_scores(q, kk, b, a_heads, row, col, DEC_SEQ)

    rsel = _div2(_iota2((rows_n, 1), 0), DEC_SEQ)
    for h in range(N_HEADS_B):
        hs = slice(h * EXPAND_B, (h + 1) * EXPAND_B)
        intra = _dot(a_heads[h].astype(BF16), v16[:, hs])
        dec_t = jnp.exp(_dot3(pick, b[:, hs])).T
        k_out_t = k_out[:, hs].T.astype(BF16)
        for bi in range(nbatch):
            rs = slice(bi * DEC_SEQ, (bi + 1) * DEC_SEQ)
            s0 = s0_ref[bi, h]
            inter = _dot(q_in[rs, hs], s0.astype(BF16))
            o_ref[rs, hs] = inter + intra[rs, :]
            v_b = jnp.where(rsel == bi, v16[:, hs], jnp.zeros_like(v16[:, hs]))
            upd = _dot(k_out_t, v_b)
            s_ref[bi, h] = s0 * dec_t[:, bi:bi + 1] + upd


def _hgrn_sample(qb, fb, ib, lb_param, s0, nbatch):
    b = s0.shape[0]
    rows_n = nbatch * DEC_SEQ
    blk = pl.BlockSpec((rows_n, D_F), lambda i: (i, 0))
    sblk = pl.BlockSpec((nbatch, N_HEADS_B, EXPAND_B, HEAD_V_B), lambda i: (i, 0, 0, 0))
    return pl.pallas_call(
        functools.partial(_hgrn_sample_body, nbatch=nbatch),
        grid=(b // nbatch,),
        in_specs=[blk, blk, blk, pl.BlockSpec((DEPTH + 1, D_F), lambda i: (0, 0)), sblk],
        out_specs=[blk, sblk],
        out_shape=[jax.ShapeDtypeStruct((b * DEC_SEQ, D_B), F32),
                   jax.ShapeDtypeStruct(s0.shape, F32)],
        compiler_params=_cparams(("arbitrary",), 32),
        name="hgrn_sample",
    )(qb, fb, ib, lb_param, s0)


EXT_KEYS = WINDOW_MAX + LANES
SHIFT_ROWS = 64


def _sample_key_tables():
    t = np.arange(DEC_SEQ)[:, None]
    e = np.arange(EXT_KEYS)[None, :]
    is_new = e >= WINDOW_MAX
    d = np.where(is_new, t - (e - WINDOW_MAX), WINDOW_MAX + t - e)
    masks = np.full((len(DILATIONS), DEC_SEQ, EXT_KEYS), -np.inf, np.float32)
    for ci, dil in enumerate(DILATIONS):
        ok = (d >= 0) & (d % dil == 0) & (d <= WIN_STEPS * dil) & (e < WINDOW_MAX + DEC_SEQ)
        masks[ci][ok] = 0.0
    return np.maximum(d, 0), masks


def _attn_sample_body(q_ref, kn_ref, vn_ref, ck_ref, cv_ref, base_ref, mask_ref, attn_ref, ok_ref, ov_ref):
    keep = LANES - DEC_SEQ
    lane = _iota2((SHIFT_ROWS, LANES), 1)
    pad_rows = jnp.zeros((keep, D_A), F32)
    for c_ref, n_ref, o_ref in ((ck_ref, kn_ref, ok_ref), (cv_ref, vn_ref, ov_ref)):
        new_t = jnp.concatenate([pad_rows, n_ref[...]], axis=0).T
        for r0 in range(0, D_A, SHIFT_ROWS):
            rows = slice(r0, r0 + SHIFT_ROWS)
            rolled = pltpu.roll(c_ref[rows, :], WINDOW_MAX - DEC_SEQ, axis=1)
            o_ref[rows, 0:WINDOW_MAX - LANES] = rolled[:, 0:WINDOW_MAX - LANES]
            o_ref[rows, WINDOW_MAX - LANES:] = jnp.where(lane >= keep, new_t[rows, :], rolled[:, WINDOW_MAX - LANES:])

    qs = q_ref[...] * ATTN_SCALE
    pad_new = jnp.zeros((keep, HEAD_DIM_A), F32)
    for h in range(N_HEADS_A):
        hs = slice(HEAD_DIM_A * h, HEAD_DIM_A * (h + 1))
        qh = qs[:, hs].astype(BF16)
        s_cache = _dot(qh, ck_ref[hs, :].astype(BF16))
        kn_pad = jnp.concatenate([kn_ref[:, hs], pad_new], axis=0).astype(BF16)
        s_new = lax.dot_general(qh, kn_pad, NT_DIMS, preferred_element_type=F32)
        s = jnp.concatenate([s_cache, s_new], axis=1) + base_ref[h]
        probs, lses = [], []
        for ci in range(len(DILATIONS)):
            sc = s + mask_ref[ci]
            m = jnp.max(sc, axis=1, keepdims=True)
            p = jnp.exp(sc - m)
            den = jnp.sum(p, axis=1, keepdims=True)
            probs.append(p / den)
            lses.append(m + jnp.log(den))
        lmax = jnp.maximum(jnp.maximum(lses[0], lses[1]), lses[2])
        ws = [jnp.exp(l - lmax) for l in lses]
        wsum = ws[0] + ws[1] + ws[2]
        pw = (probs[0] * (ws[0] / wsum) + probs[1] * (ws[1] / wsum) + probs[2] * (ws[2] / wsum)).astype(BF16)
        o = lax.dot_general(pw[:, 0:WINDOW_MAX], cv_ref[hs, :].astype(BF16), NT_DIMS, preferred_element_type=F32)
        vn_pad = jnp.concatenate([vn_ref[:, hs], pad_new], axis=0).astype(BF16)
        attn_ref[:, hs] = o + _dot(pw[:, WINDOW_MAX:], vn_pad)


def _attn_sample(q, kn, vn, cache_kt, cache_vt, base, masks):
    b = cache_kt.shape[0]
    small = pl.BlockSpec((DEC_SEQ, D_A), lambda i: (i, 0))
    big = pl.BlockSpec((None, D_A, WINDOW_MAX), lambda i: (i, 0, 0))
    const3 = lambda a: pl.BlockSpec(a.shape, lambda i: (0, 0, 0))
    return pl.pallas_call(
        _attn_sample_body,
        grid=(b,),
        in_specs=[small, small, small, big, big, const3(base), const3(masks)],
        out_specs=[small, big, big],
        out_shape=[jax.ShapeDtypeStruct((b * DEC_SEQ, D_A), F32),
                   jax.ShapeDtypeStruct(cache_kt.shape, F32),
                   jax.ShapeDtypeStruct(cache_vt.shape, F32)],
        compiler_params=_cparams(("arbitrary",), 56),
        name="attn_sample",
    )(q, kn, vn, cache_kt, cache_vt, base, masks)


def _merge_body(*refs, n_cfg):
    x_ref = refs[0]
    o_refs = refs[1:1 + n_cfg]
    l_refs = refs[1 + n_cfg:1 + 2 * n_cfg] if n_cfg > 1 else ()
    ga_ref, ob_ref, gb_ref, wout_ref, ng_ref, lg_ref, lbias_ref, y_ref = refs[1 + n_cfg + len(l_refs):]

    if n_cfg > 1:
        et_mat = _head_expand_matrix()
        ls =[r[...] for r in l_refs]
        lmax = functools.reduce(jnp.maximum, ls)
        ws = [jnp.exp(l - lmax) for l in ls]
        wsum = functools.reduce(lambda a, c: a + c, ws)
        attn = None
        for w, o_ref in zip(ws, o_refs):
            term = _dot2_rhs(w / wsum, et_mat) * o_ref[...].astype(F32)
            attn = term if attn is None else attn + term
    else:
        attn = o_refs[0][...].astype(F32)

    a_out = attn * _silu(ga_ref[...])
    parts = [a_out.astype(BF16)]
    for h in range(N_HEADS_B):
        hs = slice(h * HEAD_V_B, (h + 1) * HEAD_V_B)
        oh = ob_ref[:, hs]
        ms = jnp.mean(oh * oh, axis=-1, keepdims=True)
        on = oh * lax.rsqrt(ms + NORM_EPS) * ng_ref[...]
        parts.append((on * _silu(gb_ref[:, hs])).astype(BF16))
    mix = jnp.concatenate(parts, axis=-1)
    z = ALPHA * x_ref[...] + _dot(mix, wout_ref[...])
    mu = jnp.mean(z, axis=-1, keepdims=True)
    zc = z - mu
    var = jnp.mean(zc * zc, axis=-1, keepdims=True)
    y_ref[...] = zc * lax.rsqrt(var + NORM_EPS) * lg_ref[...] + lbias_ref[...]


def _merge(x2d, attn_outs, lses, ga, ob, gb, w_out_bf16, norm_g, ln_g, ln_b, tm):
    n = x2d.shape[0]
    n_cfg = len(attn_outs)
    row = lambda width: pl.BlockSpec((tm, width), lambda i: (i, 0))
    const = lambda shape: pl.BlockSpec(shape, lambda i: (0,) * len(shape))
    in_specs = ([row(D_MODEL)] + [row(D_A)] * n_cfg + [row(LANES)] * len(lses)
                + [row(D_A), row(D_B), row(D_B), const((D_MODEL, D_MODEL)),
                   const((1, HEAD_V_B)), const((1, D_MODEL)), const((1, D_MODEL))])
    return pl.pallas_call(
        functools.partial(_merge_body, n_cfg=n_cfg),
        grid=(n // tm,),
        in_specs=in_specs,
        out_specs=row(D_MODEL),
        out_shape=jax.ShapeDtypeStruct((n, D_MODEL), F32),
        compiler_params=_cparams(("arbitrary",), 48),
        name=f"merge_{n_cfg}",
    )(x2d, *attn_outs, *lses, ga, ob, gb, w_out_bf16, norm_g, ln_g, ln_b)


def _bias_by_distance(rel_bias, dist):
    onehot = jax.nn.one_hot(_rel_buckets(dist.reshape(-1)), NUM_BUCKETS, dtype=F32)
    table = jnp.dot(onehot, rel_bias.astype(F32), precision=lax.Precision.HIGHEST)
    return table.reshape(dist.shape + (N_HEADS_A,))


def _prompt_bias(rel_bias, dil):
    i = np.arange(ATTN_BLOCK)[:, None]
    j = np.arange(2 * ATTN_BLOCK)[None, :]
    sub = ATTN_BLOCK + i - j
    ok = (sub >= 0) & (sub <= WIN_STEPS)
    bias = _bias_by_distance(rel_bias, dil * np.clip(sub, 0, WIN_STEPS))
    bias = jnp.where(ok[:, :, None], bias, -jnp.inf)
    return jnp.transpose(bias, (2, 0, 1))


def _to_feature_major(cache):
    b, p, h, d = cache.shape
    return jnp.transpose(cache, (0, 2, 3, 1)).reshape(b, h * d, p)


def _from_feature_major(cache_t):
    b, _, p = cache_t.shape
    return jnp.transpose(cache_t.reshape(b, N_HEADS_A, HEAD_DIM_A, p), (0, 3, 1, 2))


def kernel(x_prompt, x_sample, cache_k, cache_v, state_hgrn, w_in, w_out, rel_bias, lb_param, hgrn_norm_g, ln_g, ln_b):
    bsz, seq, _ = x_prompt.shape
    dec_b, dec_t, _ = x_sample.shape
    assert dec_t == DEC_SEQ and cache_k.shape[2] == WINDOW_MAX and w_in.shape[0] == DEPTH
    keep = min(WINDOW_MAX, seq)

    w_in16 = w_in[0].astype(BF16)
    w_out16 = w_out[0].astype(BF16)
    norm_g = hgrn_norm_g[0].reshape(1, HEAD_V_B).astype(F32)
    lng = ln_g[0].reshape(1, D_MODEL).astype(F32)
    lnb = ln_b[0].reshape(1, D_MODEL).astype(F32)
    lbp = lb_param.astype(F32)

    tm = 512
    n_p = bsz * seq
    tiles_per_seq = seq // tm
    win_tiles = keep // tm
    main = lambda i: (i, 0)
    win = lambda i: ((i // tiles_per_seq) * win_tiles + jnp.maximum(i % tiles_per_seq - (tiles_per_seq - win_tiles), 0), 0)
    o_qa, o_ka, o_va, o_ga, o_qb, o_fb, o_ib, o_gb = (D_A * 0, D_A * 1, D_A * 2, D_A * 3, 4 * D_A,
                                                    4 * D_A + D_F, 4 * D_A + 2 * D_F, 4 * D_A + 2 * D_F + D_B)
    outs = [
        (o_qa, D_A, ATTN_SCALE, BF16, n_p, main),
        (o_ka, D_A, 1.0, BF16, n_p, main),
        (o_va, D_A, 1.0, BF16, n_p, main),
        (o_ka, D_A, 1.0, F32, bsz * keep, win),
        (o_va, D_A, 1.0, F32, bsz * keep, win),
        (o_ga, D_A, 1.0, F32, n_p, main),
        (o_qb, D_F, 1.0, F32, n_p, main),
        (o_fb, D_F, 1.0, F32, n_p, main),
        (o_ib, D_B, 1.0, F32, n_p, main),
        (o_gb, D_B, 1.0, F32, n_p, main),
    ]
    x2d = x_prompt.reshape(n_p, D_MODEL)
    qa, ka, va, kwin, vwin, ga, qb, fb, ib, gb = _proj(x2d, w_in16, outs, tm)

    attn_outs, lses = [], []
    for dil in DILATIONS:
        o, lse = _attn_prompt(qa, ka, va, _prompt_bias(rel_bias, dil), dil, bsz, seq)
        attn_outs.append(o)
        lses.append(lse)
    ob, s_prompt = _hgrn_prompt(qb, fb, ib, lbp, bsz, seq, 512)
    y_prompt = _merge(x2d, attn_outs, lses, ga, ob, gb, w_out16, norm_g, lng, lnb, tm).reshape(bsz, seq, D_MODEL)

    n_s = dec_b * dec_t
    outs_s = [(lo, D_A, 1.0, F32, n_s, main) for lo in (o_qa, o_ka, o_va, o_ga, o_qb, o_fb, o_ib, o_gb)]
    xs2d = x_sample.reshape(n_s, D_MODEL)
    tm_s = min(tm, n_s)
    sqa, ska, sva, sga, sqb, sfb, sib, sgb = _proj(xs2d, w_in16, outs_s, tm_s)
    dist_s, masks_s = _sample_key_tables()
    base_s = jnp.transpose(_bias_by_distance(rel_bias, dist_s), (2, 0, 1))
    attn_s, k_new_t, v_new_t = _attn_sample(sqa, ska, sva, _to_feature_major(cache_k[0]), _to_feature_major(cache_v[0]),
                                            base_s, jnp.asarray(masks_s))
    ob_s, s_sample = _hgrn_sample(sqb, sfb, sib, lbp, state_hgrn[0], LANES // DEC_SEQ)
    y_sample = _merge(xs2d, [attn_s], [], sga, ob_s, sgb, w_out16, norm_g, lng, lnb, tm_s).reshape(dec_b, dec_t, D_MODEL)

    win_shape = (DEPTH, bsz, keep, N_HEADS_A, HEAD_DIM_A)
    return (y_prompt, y_sample,
            kwin.reshape(win_shape), vwin.reshape(win_shape),
            _from_feature_major(k_new_t)[None], _from_feature_major(v_new_t)[None],
            s_prompt[None], s_sample[None].astype(state_hgrn.dtype))
```

```python
import functools

import numpy as np
import jax
import jax.numpy as jnp
from jax import lax
from jax.experimental import pallas as pl
from jax.experimental.pallas import tpu as pltpu

F32 = jnp.float32
BF16 = jnp.bfloat16

D_MODEL = 1024
D_A = 512
HEAD_DIM_A = 64
N_HEADS_A = 8
DILATIONS = (1, 4, 16)
WIN_STEPS = 128
WINDOW_MAX = 2048
ATTN_BLOCK = 128
ATTN_SCALE = HEAD_DIM_A ** -0.5
NUM_BUCKETS = 32
REL_MAX_DIST = 2048
D_B = 512
N_HEADS_B = 4
HEAD_V_B = 128
EXPAND_B = 128
D_F = 512
GLA_CHUNK = 64
D_IN = 4 * D_A + 2 * D_F + 2 * D_B
DEPTH = 1
ALPHA = (2.0 * DEPTH) ** 0.25
NORM_EPS = 1e-5
DEC_SEQ = 8
LOG2_E = 1.4426950408889634
LN_2 = 0.6931471805599453

LANES = 128
MIB = 1024 * 1024

NT_DIMS = (((1,), (1,)), ((), ()))
TN_DIMS = (((0,), (0,)), ((), ()))


def _cparams(semantics, vmem_mib):
    return pltpu.CompilerParams(dimension_semantics=semantics, vmem_limit_bytes=vmem_mib * MIB)


def _rel_buckets(dist):
    max_exact = NUM_BUCKETS // 2
    d = np.maximum(dist, 1).astype(np.float32)
    large = max_exact + (np.log(d / max_exact) / np.log(REL_MAX_DIST / max_exact)
                         * (NUM_BUCKETS - max_exact)).astype(np.int32)
    large = np.minimum(large, NUM_BUCKETS - 1)
    return np.where(dist < max_exact, dist, large).astype(np.int32)


def _split3(x):
    hi = x.astype(BF16)
    r1 = x - hi.astype(F32)
    mid = r1.astype(BF16)
    lo = (r1 - mid.astype(F32)).astype(BF16)
    return hi, mid, lo


def _dot(a, b):
    return jnp.dot(a, b, preferred_element_type=F32)


def _dot3(mat_bf16, x):
    hi, mid, lo = _split3(x)
    return _dot(mat_bf16, hi) + _dot(mat_bf16, mid) + _dot(mat_bf16, lo)


def _split2(x):
    hi = x.astype(BF16)
    return hi, (x - hi.astype(F32)).astype(BF16)


def _dot2_rhs(x, mat_bf16):
    hi, lo = _split2(x)
    return _dot(hi, mat_bf16) + _dot(lo, mat_bf16)


def _dot2_lhs(mat_bf16, x):
    hi, lo = _split2(x)
    return _dot(mat_bf16, hi) + _dot(mat_bf16, lo)


def _sigmoid(x):
    return 0.5 * jnp.tanh(0.5 * x) + 0.5


def _silu(x):
    return x * _sigmoid(x)


def _bcast_rows(x, block, j):
    rows = x.shape[0]
    parts = [jnp.broadcast_to(x[r0 + j:r0 + j + 1, :], (block, x.shape[1])) for r0 in range(0, rows, block)]
    return parts[0] if len(parts) == 1 else jnp.concatenate(parts, axis=0)


def _proj_body(x_ref, w_ref, *out_refs, groups):
    x = x_ref[...].astype(BF16)
    for (lo, width, scale, idxs) in groups:
        h = _dot(x, w_ref[:, lo:lo + width])
        if scale != 1.0:
            h = h * scale
        for i, transposed in idxs:
            out_refs[i][...] = (h.T if transposed else h).astype(out_refs[i].dtype)


def _proj(x2d, w_bf16, outs, tm):
    n = x2d.shape[0]
    groups = {}
    for i, (lo, width, scale, _, rows, _) in enumerate(outs):
        groups.setdefault((lo, width, scale), []).append((i, isinstance(rows, tuple)))
    groups = tuple((lo, width, scale, tuple(idxs)) for (lo, width, scale), idxs in groups.items())
    out_shape, out_specs = [], []
    for (_, width, _, dt, rows, imap) in outs:
        if isinstance(rows, tuple):
            out_shape.append(jax.ShapeDtypeStruct((rows[0], width, rows[1]), dt))
            out_specs.append(pl.BlockSpec((None, width, tm), imap))
        else:
            out_shape.append(jax.ShapeDtypeStruct((rows, width), dt))
            out_specs.append(pl.BlockSpec((tm, width), imap))
    return pl.pallas_call(
        functools.partial(_proj_body, groups=groups),
        grid=(n // tm,),
        in_specs=[pl.BlockSpec((tm, D_MODEL), lambda i: (i, 0)),
                  pl.BlockSpec((D_MODEL, D_IN), lambda i: (0, 0))],
        out_specs=out_specs,
        out_shape=out_shape,
        compiler_params=_cparams(("arbitrary",), 56),
        name="proj",
    )(x2d, w_bf16)


ATT_TILE = WIN_STEPS * max(DILATIONS)
N_PAIRS = D_A // LANES


def _rows(start, dil):
    return pl.ds(start, ATTN_BLOCK) if dil == 1 else pl.ds(start, ATTN_BLOCK, stride=dil)


def _attn_prompt_body(q_ref, k_ref, v_ref, bias_ref, o_ref, lse_ref, qs, kring, vring, os_, ls, *, dil):
    t = pl.program_id(1)
    slot = lax.rem(t, 2)
    cur_base = pl.multiple_of(slot * ATT_TILE, ATT_TILE)
    other_base = pl.multiple_of((1 - slot) * ATT_TILE, ATT_TILE)
    for j in range(N_PAIRS):
        lanes = slice(LANES * j, LANES * (j + 1))
        qs[j] = q_ref[:, lanes].astype(F32)
        kring[j, pl.ds(cur_base, ATT_TILE), :] = k_ref[:, lanes].astype(F32)
        vring[j, pl.ds(cur_base, ATT_TILE), :] = v_ref[:, lanes].astype(F32)

    @pl.when(t == 0)
    def _():
        zeros = jnp.zeros((ATT_TILE, LANES), F32)
        for j in range(N_PAIRS):
            kring[j, pl.ds(other_base, ATT_TILE), :] = zeros
            vring[j, pl.ds(other_base, ATT_TILE), :] = zeros

    lane = _iota2((ATTN_BLOCK, LANES), 1)
    low_half = lane < HEAD_DIM_A
    span = dil * ATTN_BLOCK

    def unit(u, carry):
        r = lax.rem(u, dil)
        n = lax.div(u, dil)
        off = r + span * n
        prev_start = jnp.where(n >= 1, cur_base + off - span, other_base + ATT_TILE - span + r)
        bias_set = jnp.where(jnp.logical_and(t == 0, n == 0), N_HEADS_A, 0)
        q_rows = _rows(off, dil)
        cur_rows = _rows(cur_base + off, dil)
        prev_rows = _rows(prev_start, dil)

        scores = []
        for j in range(N_PAIRS):
            qj = qs[j, q_rows, :]
            kcat = jnp.concatenate([kring[j, prev_rows, :], kring[j, cur_rows, :]], axis=0).astype(BF16)
            for qh in (jnp.where(low_half, qj, 0.0), jnp.where(low_half, 0.0, qj)):
                scores.append(lax.dot_general(qh.astype(BF16), kcat, NT_DIMS, preferred_element_type=F32))

        probs, inv_dens = [], []
        lse_tile = jnp.zeros((ATTN_BLOCK, LANES), F32)
        for h in range(N_HEADS_A):
            s = scores[h] + bias_ref[bias_set + h]
            m = jnp.max(jnp.maximum(s[:, :ATTN_BLOCK], s[:, ATTN_BLOCK:]), axis=1, keepdims=True)
            p = jnp.exp2(s - m)
            den = jnp.sum(p[:, :ATTN_BLOCK] + p[:, ATTN_BLOCK:], axis=1, keepdims=True)
            probs.append(p.astype(BF16))
            inv_dens.append(1.0 / den)
            lse_tile = jnp.where(lane == h, (m + jnp.log2(den)) * LN_2, lse_tile)

        for j in range(N_PAIRS):
            vcat = jnp.concatenate([vring[j, prev_rows, :], vring[j, cur_rows, :]], axis=0).astype(BF16)
            o_a = _dot(probs[2 * j], vcat) * inv_dens[2 * j]
            o_b = _dot(probs[2 * j + 1], vcat) * inv_dens[2 * j + 1]
            os_[j, q_rows, :] = jnp.where(low_half, o_a, o_b)
        ls[q_rows, :] = lse_tile
        return carry

    lax.fori_loop(0, ATT_TILE // ATTN_BLOCK, unit, 0)

    for j in range(N_PAIRS):
        o_ref[:, LANES * j:LANES * (j + 1)] = os_[j].astype(o_ref.dtype)
    lse_ref[...] = ls[...]


def _attn_prompt(q, k, v, bias, dil, b, t):
    nt = t // ATT_TILE
    blk = pl.BlockSpec((ATT_TILE, D_A), lambda bi, ti: (bi * nt + ti, 0))
    lse_blk = pl.BlockSpec((ATT_TILE, LANES), lambda bi, ti: (bi * nt + ti, 0))
    slab = lambda rows: pltpu.VMEM((N_PAIRS, rows, LANES), F32)
    return pl.pallas_call(
        functools.partial(_attn_prompt_body, dil=dil),
        grid=(b, nt),
        in_specs=[blk, blk, blk,
                  pl.BlockSpec((2 * N_HEADS_A, ATTN_BLOCK, 2 * ATTN_BLOCK), lambda bi, ti: (0, 0, 0))],
        out_specs=[blk, lse_blk],
        out_shape=[jax.ShapeDtypeStruct((b * t, D_A), BF16),
                   jax.ShapeDtypeStruct((b * t, LANES), F32)],
        scratch_shapes=[slab(ATT_TILE), slab(2 * ATT_TILE), slab(2 * ATT_TILE), slab(ATT_TILE),
                        pltpu.VMEM((ATT_TILE, LANES), F32)],
        compiler_params=_cparams(("arbitrary", "arbitrary"), 56),
        name=f"attn_prompt_d{dil}",
    )(q, k, v, bias)


def _lower_bound(lbp_ref):
    p0 = lbp_ref[0:1, :]
    p1 = lbp_ref[1:2, :]
    m = jnp.maximum(p0, p1)
    e0 = jnp.exp(p0 - m)
    e1 = jnp.exp(p1 - m)
    return e0 / (e0 + e1)


def _gates(qb, fb, lb):
    f = lb + (1.0 - lb) * _sigmoid(fb)
    return _silu(qb) * (EXPAND_B ** -0.5), 1.0 - f, jnp.log(f)


def _iota2(shape, dim):
    return lax.broadcasted_iota(jnp.int32, shape, dim)


def _div2(x, n):
    return jnp.right_shift(x, int(n).bit_length() - 1)


def _mod2(x, n):
    return jnp.bitwise_and(x, n - 1)


def _head_expand_matrix():
    return (_iota2((LANES, D_A), 0) == _div2(_iota2((LANES, D_A), 1), HEAD_DIM_A)).astype(BF16)


def _diag_block_scores(q, kk, b, a_heads, row, col, block):
    same = _div2(row, block) == _div2(col, block)
    for j in range(block):
        bj = _bcast_rows(b, block, j)
        kj = _bcast_rows(kk, block, j)
        x = q * jnp.exp(b - bj) * kj
        sel = same & (_mod2(col, block) == j) & (_mod2(row, block) >= j)
        for h in range(N_HEADS_B):
            cs = jnp.sum(x[:, h * EXPAND_B:(h + 1) * EXPAND_B], axis=-1, keepdims=True)
            a_heads[h] = jnp.where(sel, cs, a_heads[h])
    return a_heads


def _hgrn_prompt_body(qb_ref, fb_ref, ib_ref, lbp_ref, o_ref, s_ref,
                      st_ref, q_s, kk_s, b_s, ref_s, qin_s, kout_s, v_s, a_s, *, n_chunks):
    t = pl.program_id(1)
    c_len = GLA_CHUNK
    chunks = [slice(c * c_len, (c + 1) * c_len) for c in range(n_chunks)]
    heads = [slice(h * EXPAND_B, (h + 1) * EXPAND_B) for h in range(N_HEADS_B)]

    @pl.when(t == 0)
    def _():
        st_ref[...] = jnp.zeros_like(st_ref)

    lb = _lower_bound(lbp_ref)
    row = _iota2((c_len, c_len), 0)
    col = _iota2((c_len, c_len), 1)
    tri = (col <= row).astype(BF16)
    half_sizes = (32, 16, 8, 4, 2, 1)
    masks, picks = [], []
    for n in half_sizes:
        masks.append((_div2(row, 2 * n) == _div2(col, 2 * n)) & (_mod2(_div2(row, n), 2) == 1)
                     & (_mod2(_div2(col, n), 2) == 0))
        picks.append((col == _div2(row, 2 * n) * (2 * n) + (n - 1)).astype(BF16))
    pick_all = jnp.concatenate(picks, axis=0)
    on_diag = row == col

    for c, cs in enumerate(chunks):
        q, kk, g = _gates(qb_ref[cs, :], fb_ref[cs, :], lb)
        q_s[cs, :] = q
        kk_s[cs, :] = kk
        v_s[cs, :] = ib_ref[cs, :].astype(BF16)
        b_c = _dot2_lhs(tri, g)
        b_s[cs, :] = b_c
        refs = _dot(pick_all, b_c.astype(BF16))
        for li in range(len(half_sizes)):
            ref_s[li, cs, :] = refs[li * c_len:(li + 1) * c_len, :]
        b_end = b_c[c_len - 1:c_len, :]
        qin_s[cs, :] = (q * jnp.exp(b_c)).astype(BF16)
        kout_s[cs, :] = (kk * jnp.exp(b_end - b_c)).astype(BF16)
        qk = q * kk
        for h, hs in enumerate(heads):
            a_s[c, h] = jnp.where(on_diag, jnp.sum(qk[:, hs], axis=-1, keepdims=True), 0.0)
    for li, mask in enumerate(masks):
        for c, cs in enumerate(chunks):
            d = b_s[cs, :] - ref_s[li, cs, :]
            ql = (q_s[cs, :] * jnp.exp(d)).astype(BF16)
            kl = (kk_s[cs, :] * jnp.exp(-d)).astype(BF16)
            for h, hs in enumerate(heads):
                part = lax.dot_general(ql[:, hs], kl[:, hs], NT_DIMS, preferred_element_type=F32)
                a_s[c, h] = jnp.where(mask, part, a_s[c, h])

    for c, cs in enumerate(chunks):
        dec = jnp.exp(b_s[c * c_len + c_len - 1:(c + 1) * c_len, :])
        for h, hs in enumerate(heads):
            state = st_ref[h]
            inter = lax.dot_general(qin_s[cs, hs], state.astype(BF16), NT_DIMS, preferred_element_type=F32)
            intra = _dot(a_s[c, h].astype(BF16), v_s[cs, hs])
            o_ref[cs, hs] = (inter + intra).astype(o_ref.dtype)
            upd = lax.dot_general(v_s[cs, hs], kout_s[cs, hs], TN_DIMS, preferred_element_type=F32)
            st_ref[h] = state * dec[:, hs] + upd

    @pl.when(t == pl.num_programs(1) - 1)
    def _():
        for h in range(N_HEADS_B):
            s_ref[h] = st_ref[h].T


def _hgrn_prompt(qb, fb, ib, lb_param, b, t, tt):
    nt = t // tt
    n_chunks = tt // GLA_CHUNK
    blk = pl.BlockSpec((tt, D_F), lambda bi, ti: (bi * nt + ti, 0))
    tile = lambda dt: pltpu.VMEM((tt, D_F), dt)
    return pl.pallas_call(
        functools.partial(_hgrn_prompt_body, n_chunks=n_chunks),
        grid=(b, nt),
        in_specs=[blk, blk, blk, pl.BlockSpec((DEPTH + 1, D_F), lambda bi, ti: (0, 0))],
        out_specs=[blk,
                   pl.BlockSpec((None, N_HEADS_B, EXPAND_B, HEAD_V_B), lambda bi, ti: (bi, 0, 0, 0))],
        out_shape=[jax.ShapeDtypeStruct((b * t, D_B), BF16),
                   jax.ShapeDtypeStruct((b, N_HEADS_B, EXPAND_B, HEAD_V_B), F32)],
        scratch_shapes=[pltpu.VMEM((N_HEADS_B, HEAD_V_B, EXPAND_B), F32),
                        tile(F32), tile(F32), tile(F32),
                        pltpu.VMEM((6, tt, D_F), F32),
                        tile(BF16), tile(BF16), tile(BF16),
                        pltpu.VMEM((n_chunks, N_HEADS_B, GLA_CHUNK, GLA_CHUNK), F32)],
        compiler_params=_cparams(("arbitrary", "arbitrary"), 40),
        name="hgrn_prompt",
    )(qb, fb, ib, lb_param)


def _hgrn_sample_body(qb_ref, fb_ref, ib_ref, lbp_ref, s0_ref, o_ref, s_ref, *, nbatch):
    rows_n = nbatch * DEC_SEQ
    assert rows_n == LANES
    lb = _lower_bound(lbp_ref)
    row = _iota2((rows_n, rows_n), 0)
    col = _iota2((rows_n, rows_n), 1)
    same = _div2(row, DEC_SEQ) == _div2(col, DEC_SEQ)
    tri = (same & (col <= row)).astype(BF16)
    pick = (col == row * DEC_SEQ + (DEC_SEQ - 1)).astype(BF16)

    q, kk, g = _gates(qb_ref[...], fb_ref[...], lb)
    v = ib_ref[...]
    b = _dot3(tri, g)
    b_end = _bcast_rows(b, DEC_SEQ, DEC_SEQ - 1)
    q_in = (q * jnp.exp(b)).astype(BF16)
    k_out = kk * jnp.exp(b_end - b)
    v16 = v.astype(BF16)

    a_heads = [jnp.zeros((rows_n, rows_n), F32) for _ in range(N_HEADS_B)]
    a_heads = _diag_block_scores(q, kk, b, a_heads, row, col, DEC_SEQ)

    rsel = _div2(_iota2((rows_n, 1), 0), DEC_SEQ)
    for h in range(N_HEADS_B):
        hs = slice(h * EXPAND_B, (h + 1) * EXPAND_B)
        intra = _dot(a_heads[h].astype(BF16), v16[:, hs])
        dec_t = jnp.exp(_dot3(pick, b[:, hs])).T
        k_out_t = k_out[:, hs].T.astype(BF16)
        for bi in range(nbatch):
            rs = slice(bi * DEC_SEQ, (bi + 1) * DEC_SEQ)
            s0 = s0_ref[bi, h]
            inter = _dot(q_in[rs, hs], s0.astype(BF16))
            o_ref[rs, hs] = inter + intra[rs, :]
            v_b = jnp.where(rsel == bi, v16[:, hs], jnp.zeros_like(v16[:, hs]))
            upd = _dot(k_out_t, v_b)
            s_ref[bi, h] = s0 * dec_t[:, bi:bi + 1] + upd


def _hgrn_sample(qb, fb, ib, lb_param, s0, nbatch):
    b = s0.shape[0]
    rows_n = nbatch * DEC_SEQ
    blk = pl.BlockSpec((rows_n, D_F), lambda i: (i, 0))
    sblk = pl.BlockSpec((nbatch, N_HEADS_B, EXPAND_B, HEAD_V_B), lambda i: (i, 0, 0, 0))
    return pl.pallas_call(
        functools.partial(_hgrn_sample_body, nbatch=nbatch),
        grid=(b // nbatch,),
        in_specs=[blk, blk, blk, pl.BlockSpec((DEPTH + 1, D_F), lambda i: (0, 0)), sblk],
        out_specs=[blk, sblk],
        out_shape=[jax.ShapeDtypeStruct((b * DEC_SEQ, D_B), F32),
                   jax.ShapeDtypeStruct(s0.shape, F32)],
        compiler_params=_cparams(("arbitrary",), 32),
        name="hgrn_sample",
    )(qb, fb, ib, lb_param, s0)


EXT_KEYS = WINDOW_MAX + LANES
SHIFT_ROWS = 64


def _sample_key_tables():
    t = np.arange(DEC_SEQ)[:, None]
    e = np.arange(EXT_KEYS)[None, :]
    is_new = e >= WINDOW_MAX
    d = np.where(is_new, t - (e - WINDOW_MAX), WINDOW_MAX + t - e)
    masks = np.full((len(DILATIONS), DEC_SEQ, EXT_KEYS), -np.inf, np.float32)
    for ci, dil in enumerate(DILATIONS):
        ok = (d >= 0) & (d % dil == 0) & (d <= WIN_STEPS * dil) & (e < WINDOW_MAX + DEC_SEQ)
        masks[ci][ok] = 0.0
    return np.maximum(d, 0), masks


def _attn_sample_body(q_ref, kn_ref, vn_ref, ck_ref, cv_ref, base_ref, mask_ref, attn_ref, ok_ref, ov_ref):
    keep = LANES - DEC_SEQ
    lane = _iota2((SHIFT_ROWS, LANES), 1)
    pad_rows = jnp.zeros((keep, D_A), F32)
    for c_ref, n_ref, o_ref in ((ck_ref, kn_ref, ok_ref), (cv_ref, vn_ref, ov_ref)):
        new_t = jnp.concatenate([pad_rows, n_ref[...]], axis=0).T
        for r0 in range(0, D_A, SHIFT_ROWS):
            rows = slice(r0, r0 + SHIFT_ROWS)
            rolled = pltpu.roll(c_ref[rows, :], WINDOW_MAX - DEC_SEQ, axis=1)
            o_ref[rows, 0:WINDOW_MAX - LANES] = rolled[:, 0:WINDOW_MAX - LANES]
            o_ref[rows, WINDOW_MAX - LANES:] = jnp.where(lane >= keep, new_t[rows, :], rolled[:, WINDOW_MAX - LANES:])

    qs = q_ref[...] * ATTN_SCALE
    pad_new = jnp.zeros((keep, HEAD_DIM_A), F32)
    for h in range(N_HEADS_A):
        hs = slice(HEAD_DIM_A * h, HEAD_DIM_A * (h + 1))
        qh = qs[:, hs].astype(BF16)
        s_cache = _dot(qh, ck_ref[hs, :].astype(BF16))
        kn_pad = jnp.concatenate([kn_ref[:, hs], pad_new], axis=0).astype(BF16)
        s_new = lax.dot_general(qh, kn_pad, NT_DIMS, preferred_element_type=F32)
        s = jnp.concatenate([s_cache, s_new], axis=1) + base_ref[h]
        probs, lses = [], []
        for ci in range(len(DILATIONS)):
            sc = s + mask_ref[ci]
            m = jnp.max(sc, axis=1, keepdims=True)
            p = jnp.exp(sc - m)
            den = jnp.sum(p, axis=1, keepdims=True)
            probs.append(p / den)
            lses.append(m + jnp.log(den))
        lmax = jnp.maximum(jnp.maximum(lses[0], lses[1]), lses[2])
        ws = [jnp.exp(l - lmax) for l in lses]
        wsum = ws[0] + ws[1] + ws[2]
        pw = (probs[0] * (ws[0] / wsum) + probs[1] * (ws[1] / wsum) + probs[2] * (ws[2] / wsum)).astype(BF16)
        o = lax.dot_general(pw[:, 0:WINDOW_MAX], cv_ref[hs, :].astype(BF16), NT_DIMS, preferred_element_type=F32)
        vn_pad = jnp.concatenate([vn_ref[:, hs], pad_new], axis=0).astype(BF16)
        attn_ref[:, hs] = o + _dot(pw[:, WINDOW_MAX:], vn_pad)


def _attn_sample(q, kn, vn, cache_kt, cache_vt, base, masks):
    b = cache_kt.shape[0]
    small = pl.BlockSpec((DEC_SEQ, D_A), lambda i: (i, 0))
    big = pl.BlockSpec((None, D_A, WINDOW_MAX), lambda i: (i, 0, 0))
    const3 = lambda a: pl.BlockSpec(a.shape, lambda i: (0, 0, 0))
    return pl.pallas_call(
        _attn_sample_body,
        grid=(b,),
        in_specs=[small, small, small, big, big, const3(base), const3(masks)],
        out_specs=[small, big, big],
        out_shape=[jax.ShapeDtypeStruct((b * DEC_SEQ, D_A), F32),
                   jax.ShapeDtypeStruct(cache_kt.shape, F32),
                   jax.ShapeDtypeStruct(cache_vt.shape, F32)],
        compiler_params=_cparams(("arbitrary",), 56),
        name="attn_sample",
    )(q, kn, vn, cache_kt, cache_vt, base, masks)


def _merge_body(*refs, n_cfg):
    x_ref = refs[0]
    o_refs = refs[1:1 + n_cfg]
    l_refs = refs[1 + n_cfg:1 + 2 * n_cfg] if n_cfg > 1 else ()
    ga_ref, ob_ref, gb_ref, wout_ref, ng_ref, lg_ref, lbias_ref, y_ref = refs[1 + n_cfg + len(l_refs):]

    if n_cfg > 1:
        et_mat = _head_expand_matrix()
        ls =[r[...] for r in l_refs]
        lmax = functools.reduce(jnp.maximum, ls)
        ws = [jnp.exp(l - lmax) for l in ls]
        wsum = functools.reduce(lambda a, c: a + c, ws)
        attn = None
        for w, o_ref in zip(ws, o_refs):
            term = _dot2_rhs(w / wsum, et_mat) * o_ref[...].astype(F32)
            attn = term if attn is None else attn + term
    else:
        attn = o_refs[0][...].astype(F32)

    a_out = attn * _silu(ga_ref[...].astype(F32))
    parts = [a_out.astype(BF16)]
    for h in range(N_HEADS_B):
        hs = slice(h * HEAD_V_B, (h + 1) * HEAD_V_B)
        oh = ob_ref[:, hs].astype(F32)
        ms = jnp.mean(oh * oh, axis=-1, keepdims=True)
        on = oh * lax.rsqrt(ms + NORM_EPS) * ng_ref[...]
        parts.append((on * _silu(gb_ref[:, hs].astype(F32))).astype(BF16))
    mix = jnp.concatenate(parts, axis=-1)
    z = ALPHA * x_ref[...] + _dot(mix, wout_ref[...])
    mu = jnp.mean(z, axis=-1, keepdims=True)
    zc = z - mu
    var = jnp.mean(zc * zc, axis=-1, keepdims=True)
    y_ref[...] = zc * lax.rsqrt(var + NORM_EPS) * lg_ref[...] + lbias_ref[...]


def _merge(x2d, attn_outs, lses, ga, ob, gb, w_out_bf16, norm_g, ln_g, ln_b, tm):
    n = x2d.shape[0]
    n_cfg = len(attn_outs)
    row = lambda width: pl.BlockSpec((tm, width), lambda i: (i, 0))
    const = lambda shape: pl.BlockSpec(shape, lambda i: (0,) * len(shape))
    in_specs = ([row(D_MODEL)] + [row(D_A)] * n_cfg + [row(LANES)] * len(lses)
                + [row(D_A), row(D_B), row(D_B), const((D_MODEL, D_MODEL)),
                   const((1, HEAD_V_B)), const((1, D_MODEL)), const((1, D_MODEL))])
    return pl.pallas_call(
        functools.partial(_merge_body, n_cfg=n_cfg),
        grid=(n // tm,),
        in_specs=in_specs,
        out_specs=row(D_MODEL),
        out_shape=jax.ShapeDtypeStruct((n, D_MODEL), F32),
        compiler_params=_cparams(("arbitrary",), 48),
        name=f"merge_{n_cfg}",
    )(x2d, *attn_outs, *lses, ga, ob, gb, w_out_bf16, norm_g, ln_g, ln_b)


def _bias_by_distance(rel_bias, dist):
    onehot = jax.nn.one_hot(_rel_buckets(dist.reshape(-1)), NUM_BUCKETS, dtype=F32)
    table = jnp.dot(onehot, rel_bias.astype(F32), precision=lax.Precision.HIGHEST)
    return table.reshape(dist.shape + (N_HEADS_A,))


def _prompt_bias(rel_bias, dil):
    i = np.arange(ATTN_BLOCK)[:, None]
    j = np.arange(2 * ATTN_BLOCK)[None, :]
    sub = ATTN_BLOCK + i - j
    ok = (sub >= 0) & (sub <= WIN_STEPS)
    bias = _bias_by_distance(rel_bias, dil * np.clip(sub, 0, WIN_STEPS)) * LOG2_E
    full = jnp.where(ok[:, :, None], bias, -jnp.inf)
    no_prev = jnp.where((ok & (j >= ATTN_BLOCK))[:, :, None], bias, -jnp.inf)
    return jnp.transpose(jnp.concatenate([full, no_prev], axis=2), (2, 0, 1))


def _to_feature_major(cache):
    b, p, h, d = cache.shape
    return jnp.transpose(cache, (0, 2, 3, 1)).reshape(b, h * d, p)


def _from_feature_major(cache_t):
    b, _, p = cache_t.shape
    return jnp.transpose(cache_t.reshape(b, N_HEADS_A, HEAD_DIM_A, p), (0, 3, 1, 2))


def kernel(x_prompt, x_sample, cache_k, cache_v, state_hgrn, w_in, w_out, rel_bias, lb_param, hgrn_norm_g, ln_g, ln_b):
    bsz, seq, _ = x_prompt.shape
    dec_b, dec_t, _ = x_sample.shape
    assert dec_t == DEC_SEQ and cache_k.shape[2] == WINDOW_MAX and w_in.shape[0] == DEPTH
    keep = min(WINDOW_MAX, seq)

    w_in16 = w_in[0].astype(BF16)
    w_out16 = w_out[0].astype(BF16)
    norm_g = hgrn_norm_g[0].reshape(1, HEAD_V_B).astype(F32)
    lng = ln_g[0].reshape(1, D_MODEL).astype(F32)
    lnb = ln_b[0].reshape(1, D_MODEL).astype(F32)
    lbp = lb_param.astype(F32)

    tm = 512
    n_p = bsz * seq
    tiles_per_seq = seq // tm
    win_tiles = keep // tm
    main = lambda i: (i, 0)
    win = lambda i: (i // tiles_per_seq, 0, jnp.maximum(i % tiles_per_seq - (tiles_per_seq - win_tiles), 0))
    o_qa, o_ka, o_va, o_ga, o_qb, o_fb, o_ib, o_gb = (D_A * 0, D_A * 1, D_A * 2, D_A * 3, 4 * D_A,
                                                    4 * D_A + D_F, 4 * D_A + 2 * D_F, 4 * D_A + 2 * D_F + D_B)
    outs = [
        (o_qa, D_A, ATTN_SCALE * LOG2_E, BF16, n_p, main),
        (o_ka, D_A, 1.0, BF16, n_p, main),
        (o_va, D_A, 1.0, BF16, n_p, main),
        (o_ka, D_A, 1.0, F32, (bsz, keep), win),
        (o_va, D_A, 1.0, F32, (bsz, keep), win),
        (o_ga, D_A, 1.0, BF16, n_p, main),
        (o_qb, D_F, 1.0, F32, n_p, main),
        (o_fb, D_F, 1.0, F32, n_p, main),
        (o_ib, D_B, 1.0, F32, n_p, main),
        (o_gb, D_B, 1.0, BF16, n_p, main),
    ]
    x2d = x_prompt.reshape(n_p, D_MODEL)
    qa, ka, va, kwin, vwin, ga, qb, fb, ib, gb = _proj(x2d, w_in16, outs, tm)

    attn_outs, lses = [], []
    for dil in DILATIONS:
        o, lse = _attn_prompt(qa, ka, va, _prompt_bias(rel_bias, dil), dil, bsz, seq)
        attn_outs.append(o)
        lses.append(lse)
    ob, s_prompt = _hgrn_prompt(qb, fb, ib, lbp, bsz, seq, 512)
    y_prompt = _merge(x2d, attn_outs, lses, ga, ob, gb, w_out16, norm_g, lng, lnb, tm).reshape(bsz, seq, D_MODEL)

    n_s = dec_b * dec_t
    outs_s = [(lo, D_A, 1.0, F32, n_s, main) for lo in (o_qa, o_ka, o_va, o_ga, o_qb, o_fb, o_ib, o_gb)]
    xs2d = x_sample.reshape(n_s, D_MODEL)
    tm_s = min(tm, n_s)
    sqa, ska, sva, sga, sqb, sfb, sib, sgb = _proj(xs2d, w_in16, outs_s, tm_s)
    dist_s, masks_s = _sample_key_tables()
    base_s = jnp.transpose(_bias_by_distance(rel_bias, dist_s), (2, 0, 1))
    attn_s, k_new_t, v_new_t = _attn_sample(sqa, ska, sva, _to_feature_major(cache_k[0]), _to_feature_major(cache_v[0]),
                                            base_s, jnp.asarray(masks_s))
    ob_s, s_sample = _hgrn_sample(sqb, sfb, sib, lbp, state_hgrn[0], LANES // DEC_SEQ)
    y_sample = _merge(xs2d, [attn_s], [], sga, ob_s, sgb, w_out16, norm_g, lng, lnb, tm_s).reshape(dec_b, dec_t, D_MODEL)

    return (y_prompt, y_sample,
            _from_feature_major(kwin)[None], _from_feature_major(vwin)[None],
            _from_feature_major(k_new_t)[None], _from_feature_major(v_new_t)[None],
            s_prompt[None], s_sample[None].astype(state_hgrn.dtype))
```

```python
import functools

import numpy as np
import jax
import jax.numpy as jnp
from jax import lax
from jax.experimental import pallas as pl
from jax.experimental.pallas import tpu as pltpu

F32 = jnp.float32
BF16 = jnp.bfloat16

D_MODEL = 1024
D_A = 512
HEAD_DIM_A = 64
N_HEADS_A = 8
DILATIONS = (1, 4, 16)
WIN_STEPS = 128
WINDOW_MAX = 2048
ATTN_BLOCK = 128
ATTN_SCALE = HEAD_DIM_A ** -0.5
NUM_BUCKETS = 32
REL_MAX_DIST = 2048
D_B = 512
N_HEADS_B = 4
HEAD_V_B = 128
EXPAND_B = 128
D_F = 512
GLA_CHUNK = 64
D_IN = 4 * D_A + 2 * D_F + 2 * D_B
DEPTH = 1
ALPHA = (2.0 * DEPTH) ** 0.25
NORM_EPS = 1e-5
DEC_SEQ = 8
LOG2_E = 1.4426950408889634
LN_2 = 0.6931471805599453

LANES = 128
MIB = 1024 * 1024

NT_DIMS = (((1,), (1,)), ((), ()))
TN_DIMS = (((0,), (0,)), ((), ()))


def _cparams(semantics, vmem_mib):
    return pltpu.CompilerParams(dimension_semantics=semantics, vmem_limit_bytes=vmem_mib * MIB)


def _rel_buckets(dist):
    max_exact = NUM_BUCKETS // 2
    d = np.maximum(dist, 1).astype(np.float32)
    large = max_exact + (np.log(d / max_exact) / np.log(REL_MAX_DIST / max_exact)
                         * (NUM_BUCKETS - max_exact)).astype(np.int32)
    large = np.minimum(large, NUM_BUCKETS - 1)
    return np.where(dist < max_exact, dist, large).astype(np.int32)


def _split3(x):
    hi = x.astype(BF16)
    r1 = x - hi.astype(F32)
    mid = r1.astype(BF16)
    lo = (r1 - mid.astype(F32)).astype(BF16)
    return hi, mid, lo


def _dot(a, b):
    return jnp.dot(a, b, preferred_element_type=F32)


def _dot3(mat_bf16, x):
    hi, mid, lo = _split3(x)
    return _dot(mat_bf16, hi) + _dot(mat_bf16, mid) + _dot(mat_bf16, lo)


def _split2(x):
    hi = x.astype(BF16)
    return hi, (x - hi.astype(F32)).astype(BF16)


def _dot2_rhs(x, mat_bf16):
    hi, lo = _split2(x)
    return _dot(hi, mat_bf16) + _dot(lo, mat_bf16)


def _dot2_lhs(mat_bf16, x):
    hi, lo = _split2(x)
    return _dot(mat_bf16, hi) + _dot(mat_bf16, lo)


def _sigmoid(x):
    return 0.5 * jnp.tanh(0.5 * x) + 0.5


def _silu(x):
    return x * _sigmoid(x)


def _bcast_rows(x, block, j):
    rows = x.shape[0]
    parts = [jnp.broadcast_to(x[r0 + j:r0 + j + 1, :], (block, x.shape[1])) for r0 in range(0, rows, block)]
    return parts[0] if len(parts) == 1 else jnp.concatenate(parts, axis=0)


def _proj_body(x_ref, w_ref, *out_refs, groups):
    x = x_ref[...].astype(BF16)
    for (lo, width, scale, idxs) in groups:
        h = _dot(x, w_ref[:, lo:lo + width])
        if scale != 1.0:
            h = h * scale
        for i, transposed in idxs:
            out_refs[i][...] = (h.T if transposed else h).astype(out_refs[i].dtype)


def _proj(x2d, w_bf16, outs, tm):
    n = x2d.shape[0]
    groups = {}
    for i, (lo, width, scale, _, rows, _) in enumerate(outs):
        groups.setdefault((lo, width, scale), []).append((i, isinstance(rows, tuple)))
    groups = tuple((lo, width, scale, tuple(idxs)) for (lo, width, scale), idxs in groups.items())
    out_shape, out_specs = [], []
    for (_, width, _, dt, rows, imap) in outs:
        if isinstance(rows, tuple):
            out_shape.append(jax.ShapeDtypeStruct((rows[0], width, rows[1]), dt))
            out_specs.append(pl.BlockSpec((None, width, tm), imap))
        else:
            out_shape.append(jax.ShapeDtypeStruct((rows, width), dt))
            out_specs.append(pl.BlockSpec((tm, width), imap))
    return pl.pallas_call(
        functools.partial(_proj_body, groups=groups),
        grid=(n // tm,),
        in_specs=[pl.BlockSpec((tm, D_MODEL), lambda i: (i, 0)),
                  pl.BlockSpec((D_MODEL, D_IN), lambda i: (0, 0))],
        out_specs=out_specs,
        out_shape=out_shape,
        compiler_params=_cparams(("arbitrary",), 56),
        name="proj",
    )(x2d, w_bf16)


ATT_TILE = WIN_STEPS * max(DILATIONS)
N_PAIRS = D_A // LANES


def _rows(start, dil):
    return pl.ds(start, ATTN_BLOCK) if dil == 1 else pl.ds(start, ATTN_BLOCK, stride=dil)


def _attn_prompt_body(q_ref, k_ref, v_ref, bias_ref, o_ref, lse_ref, qs, kring, vring, os_, ls, *, dil):
    t = pl.program_id(1)
    slot = lax.rem(t, 2)
    cur_base = pl.multiple_of(slot * ATT_TILE, ATT_TILE)
    other_base = pl.multiple_of((1 - slot) * ATT_TILE, ATT_TILE)
    for j in range(N_PAIRS):
        lanes = slice(LANES * j, LANES * (j + 1))
        qs[j] = q_ref[:, lanes].astype(F32)
        kring[j, pl.ds(cur_base, ATT_TILE), :] = k_ref[:, lanes].astype(F32)
        vring[j, pl.ds(cur_base, ATT_TILE), :] = v_ref[:, lanes].astype(F32)

    @pl.when(t == 0)
    def _():
        zeros = jnp.zeros((ATT_TILE, LANES), F32)
        for j in range(N_PAIRS):
            kring[j, pl.ds(other_base, ATT_TILE), :] = zeros
            vring[j, pl.ds(other_base, ATT_TILE), :] = zeros

    lane = _iota2((ATTN_BLOCK, LANES), 1)
    low_half = lane < HEAD_DIM_A
    span = dil * ATTN_BLOCK

    def unit(u, carry):
        r = lax.rem(u, dil)
        n = lax.div(u, dil)
        off = r + span * n
        prev_start = jnp.where(n >= 1, cur_base + off - span, other_base + ATT_TILE - span + r)
        bias_set = jnp.where(jnp.logical_and(t == 0, n == 0), N_HEADS_A, 0)
        q_rows = _rows(off, dil)
        cur_rows = _rows(cur_base + off, dil)
        prev_rows = _rows(prev_start, dil)

        scores = []
        for j in range(N_PAIRS):
            qj = qs[j, q_rows, :]
            kcat = jnp.concatenate([kring[j, prev_rows, :], kring[j, cur_rows, :]], axis=0).astype(BF16)
            for qh in (jnp.where(low_half, qj, 0.0), jnp.where(low_half, 0.0, qj)):
                scores.append(lax.dot_general(qh.astype(BF16), kcat, NT_DIMS, preferred_element_type=F32))

        probs, inv_dens = [], []
        lse_tile = jnp.zeros((ATTN_BLOCK, LANES), F32)
        for h in range(N_HEADS_A):
            s = scores[h] + bias_ref[bias_set + h]
            m = jnp.max(jnp.maximum(s[:, :ATTN_BLOCK], s[:, ATTN_BLOCK:]), axis=1, keepdims=True)
            p = jnp.exp2(s - m)
            den = jnp.sum(p[:, :ATTN_BLOCK] + p[:, ATTN_BLOCK:], axis=1, keepdims=True)
            probs.append(p.astype(BF16))
            inv_dens.append(1.0 / den)
            lse_tile = jnp.where(lane == h, (m + jnp.log2(den)) * LN_2, lse_tile)

        for j in range(N_PAIRS):
            vcat = jnp.concatenate([vring[j, prev_rows, :], vring[j, cur_rows, :]], axis=0).astype(BF16)
            o_a = _dot(probs[2 * j], vcat) * inv_dens[2 * j]
            o_b = _dot(probs[2 * j + 1], vcat) * inv_dens[2 * j + 1]
            os_[j, q_rows, :] = jnp.where(low_half, o_a, o_b)
        ls[q_rows, :] = lse_tile
        return carry

    lax.fori_loop(0, ATT_TILE // ATTN_BLOCK, unit, 0, unroll=4)

    for j in range(N_PAIRS):
        o_ref[:, LANES * j:LANES * (j + 1)] = os_[j].astype(o_ref.dtype)
    lse_ref[...] = ls[...]


def _attn_prompt(q, k, v, bias, dil, b, t):
    nt = t // ATT_TILE
    blk = pl.BlockSpec((ATT_TILE, D_A), lambda bi, ti: (bi * nt + ti, 0))
    lse_blk = pl.BlockSpec((ATT_TILE, LANES), lambda bi, ti: (bi * nt + ti, 0))
    slab = lambda rows: pltpu.VMEM((N_PAIRS, rows, LANES), F32)
    return pl.pallas_call(
        functools.partial(_attn_prompt_body, dil=dil),
        grid=(b, nt),
        in_specs=[blk, blk, blk,
                  pl.BlockSpec((2 * N_HEADS_A, ATTN_BLOCK, 2 * ATTN_BLOCK), lambda bi, ti: (0, 0, 0))],
        out_specs=[blk, lse_blk],
        out_shape=[jax.ShapeDtypeStruct((b * t, D_A), BF16),
                   jax.ShapeDtypeStruct((b * t, LANES), F32)],
        scratch_shapes=[slab(ATT_TILE), slab(2 * ATT_TILE), slab(2 * ATT_TILE), slab(ATT_TILE),
                        pltpu.VMEM((ATT_TILE, LANES), F32)],
        compiler_params=_cparams(("arbitrary", "arbitrary"), 56),
        name=f"attn_prompt_d{dil}",
    )(q, k, v, bias)


def _lower_bound(lbp_ref):
    p0 = lbp_ref[0:1, :]
    p1 = lbp_ref[1:2, :]
    m = jnp.maximum(p0, p1)
    e0 = jnp.exp(p0 - m)
    e1 = jnp.exp(p1 - m)
    return e0 / (e0 + e1)


def _gates(qb, fb, lb):
    f = lb + (1.0 - lb) * _sigmoid(fb)
    return _silu(qb) * (EXPAND_B ** -0.5), 1.0 - f, jnp.log(f)


def _iota2(shape, dim):
    return lax.broadcasted_iota(jnp.int32, shape, dim)


def _div2(x, n):
    return jnp.right_shift(x, int(n).bit_length() - 1)


def _mod2(x, n):
    return jnp.bitwise_and(x, n - 1)


def _head_expand_matrix():
    return (_iota2((LANES, D_A), 0) == _div2(_iota2((LANES, D_A), 1), HEAD_DIM_A)).astype(BF16)


def _diag_block_scores(q, kk, b, a_heads, row, col, block):
    same = _div2(row, block) == _div2(col, block)
    for j in range(block):
        bj = _bcast_rows(b, block, j)
        kj = _bcast_rows(kk, block, j)
        x = q * jnp.exp(b - bj) * kj
        sel = same & (_mod2(col, block) == j) & (_mod2(row, block) >= j)
        for h in range(N_HEADS_B):
            cs = jnp.sum(x[:, h * EXPAND_B:(h + 1) * EXPAND_B], axis=-1, keepdims=True)
            a_heads[h] = jnp.where(sel, cs, a_heads[h])
    return a_heads


def _hgrn_prompt_tile(qb_ref, fb_ref, ib_ref, lbp_ref, o_ref, st_ref, q_s, kk_s, b_s, ref_s, qin_s, kout_s, v_s, a_s,
                      *, n_chunks):
    c_len = GLA_CHUNK
    chunks = [slice(c * c_len, (c + 1) * c_len) for c in range(n_chunks)]
    heads = [slice(h * EXPAND_B, (h + 1) * EXPAND_B) for h in range(N_HEADS_B)]

    lb = _lower_bound(lbp_ref)
    row = _iota2((c_len, c_len), 0)
    col = _iota2((c_len, c_len), 1)
    tri = (col <= row).astype(BF16)
    half_sizes = (32, 16, 8, 4, 2, 1)
    masks, picks = [], []
    for n in half_sizes:
        masks.append((_div2(row, 2 * n) == _div2(col, 2 * n)) & (_mod2(_div2(row, n), 2) == 1)
                     & (_mod2(_div2(col, n), 2) == 0))
        picks.append((col == _div2(row, 2 * n) * (2 * n) + (n - 1)).astype(BF16))
    pick_all = jnp.concatenate(picks, axis=0)
    on_diag = row == col

    for c, cs in enumerate(chunks):
        q, kk, g = _gates(qb_ref[cs, :], fb_ref[cs, :], lb)
        q_s[cs, :] = q
        kk_s[cs, :] = kk
        v_s[cs, :] = ib_ref[cs, :].astype(BF16)
        b_c = _dot2_lhs(tri, g)
        b_s[cs, :] = b_c
        refs = _dot(pick_all, b_c.astype(BF16))
        for li in range(len(half_sizes)):
            ref_s[li, cs, :] = refs[li * c_len:(li + 1) * c_len, :]
        b_end = b_c[c_len - 1:c_len, :]
        qin_s[cs, :] = (q * jnp.exp(b_c)).astype(BF16)
        kout_s[cs, :] = (kk * jnp.exp(b_end - b_c)).astype(BF16)
        qk = q * kk
        for h, hs in enumerate(heads):
            a_s[c, h] = jnp.where(on_diag, jnp.sum(qk[:, hs], axis=-1, keepdims=True), 0.0)
    for li, mask in enumerate(masks):
        for c, cs in enumerate(chunks):
            d = b_s[cs, :] - ref_s[li, cs, :]
            ql = (q_s[cs, :] * jnp.exp(d)).astype(BF16)
            kl = (kk_s[cs, :] * jnp.exp(-d)).astype(BF16)
            for h, hs in enumerate(heads):
                part = lax.dot_general(ql[:, hs], kl[:, hs], NT_DIMS, preferred_element_type=F32)
                a_s[c, h] = jnp.where(mask, part, a_s[c, h])

    for c, cs in enumerate(chunks):
        dec = jnp.exp(b_s[c * c_len + c_len - 1:(c + 1) * c_len, :])
        for h, hs in enumerate(heads):
            state = st_ref[h]
            inter = lax.dot_general(qin_s[cs, hs], state.astype(BF16), NT_DIMS, preferred_element_type=F32)
            intra = _dot(a_s[c, h].astype(BF16), v_s[cs, hs])
            o_ref[cs, hs] = (inter + intra).astype(o_ref.dtype)
            upd = lax.dot_general(v_s[cs, hs], kout_s[cs, hs], TN_DIMS, preferred_element_type=F32)
            st_ref[h] = state * dec[:, hs] + upd


def _hgrn_sample_body(qb_ref, fb_ref, ib_ref, lbp_ref, s0_ref, o_ref, s_ref, *, nbatch):
    rows_n = nbatch * DEC_SEQ
    assert rows_n == LANES
    lb = _lower_bound(lbp_ref)
    row = _iota2((rows_n, rows_n), 0)
    col = _iota2((rows_n, rows_n), 1)
    same = _div2(row, DEC_SEQ) == _div2(col, DEC_SEQ)
    tri = (same & (col <= row)).astype(BF16)
    pick = (col == row * DEC_SEQ + (DEC_SEQ - 1)).astype(BF16)

    q, kk, g = _gates(qb_ref[...], fb_ref[...], lb)
    v = ib_ref[...]
    b = _dot3(tri, g)
    b_end = _bcast_rows(b, DEC_SEQ, DEC_SEQ - 1)
    q_in = (q * jnp.exp(b)).astype(BF16)
    k_out = kk * jnp.exp(b_end - b)
    v16 = v.astype(BF16)

    a_heads = [jnp.zeros((rows_n, rows_n), F32) for _ in range(N_HEADS_B)]
    a_heads = _diag_block_scores(q, kk, b, a_heads, row, col, DEC_SEQ)

    rsel = _div2(_iota2((rows_n, 1), 0), DEC_SEQ)
    for h in range(N_HEADS_B):
        hs = slice(h * EXPAND_B, (h + 1) * EXPAND_B)
        intra = _dot(a_heads[h].astype(BF16), v16[:, hs])
        dec_t = jnp.exp(_dot3(pick, b[:, hs])).T
        k_out_t = k_out[:, hs].T.astype(BF16)
        for bi in range(nbatch):
            rs = slice(bi * DEC_SEQ, (bi + 1) * DEC_SEQ)
            s0 = s0_ref[bi, h]
            inter = _dot(q_in[rs, hs], s0.astype(BF16))
            o_ref[rs, hs] = inter + intra[rs, :]
            v_b = jnp.where(rsel == bi, v16[:, hs], jnp.zeros_like(v16[:, hs]))
            upd = _dot(k_out_t, v_b)
            s_ref[bi, h] = s0 * dec_t[:, bi:bi + 1] + upd


def _hgrn_sample(qb, fb, ib, lb_param, s0, nbatch):
    b = s0.shape[0]
    rows_n = nbatch * DEC_SEQ
    blk = pl.BlockSpec((rows_n, D_F), lambda i: (i, 0))
    sblk = pl.BlockSpec((nbatch, N_HEADS_B, EXPAND_B, HEAD_V_B), lambda i: (i, 0, 0, 0))
    return pl.pallas_call(
        functools.partial(_hgrn_sample_body, nbatch=nbatch),
        grid=(b // nbatch,),
        in_specs=[blk, blk, blk, pl.BlockSpec((DEPTH + 1, D_F), lambda i: (0, 0)), sblk],
        out_specs=[blk, sblk],
        out_shape=[jax.ShapeDtypeStruct((b * DEC_SEQ, D_B), F32),
                   jax.ShapeDtypeStruct(s0.shape, F32)],
        compiler_params=_cparams(("arbitrary",), 32),
        name="hgrn_sample",
    )(qb, fb, ib, lb_param, s0)


EXT_KEYS = WINDOW_MAX + LANES
SHIFT_ROWS = 64


def _sample_key_tables():
    t = np.arange(DEC_SEQ)[:, None]
    e = np.arange(EXT_KEYS)[None, :]
    is_new = e >= WINDOW_MAX
    d = np.where(is_new, t - (e - WINDOW_MAX), WINDOW_MAX + t - e)
    masks = np.full((len(DILATIONS), DEC_SEQ, EXT_KEYS), -np.inf, np.float32)
    for ci, dil in enumerate(DILATIONS):
        ok = (d >= 0) & (d % dil == 0) & (d <= WIN_STEPS * dil) & (e < WINDOW_MAX + DEC_SEQ)
        masks[ci][ok] = 0.0
    return np.maximum(d, 0), masks


def _attn_sample_body(q_ref, kn_ref, vn_ref, ck_ref, cv_ref, base_ref, mask_ref, attn_ref, ok_ref, ov_ref):
    keep = LANES - DEC_SEQ
    lane = _iota2((SHIFT_ROWS, LANES), 1)
    pad_rows = jnp.zeros((keep, D_A), F32)
    for c_ref, n_ref, o_ref in ((ck_ref, kn_ref, ok_ref), (cv_ref, vn_ref, ov_ref)):
        new_t = jnp.concatenate([pad_rows, n_ref[...]], axis=0).T
        for r0 in range(0, D_A, SHIFT_ROWS):
            rows = slice(r0, r0 + SHIFT_ROWS)
            rolled = pltpu.roll(c_ref[rows, :], WINDOW_MAX - DEC_SEQ, axis=1)
            o_ref[rows, 0:WINDOW_MAX - LANES] = rolled[:, 0:WINDOW_MAX - LANES]
            o_ref[rows, WINDOW_MAX - LANES:] = jnp.where(lane >= keep, new_t[rows, :], rolled[:, WINDOW_MAX - LANES:])

    qs = q_ref[...] * ATTN_SCALE
    pad_new = jnp.zeros((keep, HEAD_DIM_A), F32)
    for h in range(N_HEADS_A):
        hs = slice(HEAD_DIM_A * h, HEAD_DIM_A * (h + 1))
        qh = qs[:, hs].astype(BF16)
        s_cache = _dot(qh, ck_ref[hs, :].astype(BF16))
        kn_pad = jnp.concatenate([kn_ref[:, hs], pad_new], axis=0).astype(BF16)
        s_new = lax.dot_general(qh, kn_pad, NT_DIMS, preferred_element_type=F32)
        s = jnp.concatenate([s_cache, s_new], axis=1) + base_ref[h]
        probs, lses = [], []
        for ci in range(len(DILATIONS)):
            sc = s + mask_ref[ci]
            m = jnp.max(sc, axis=1, keepdims=True)
            p = jnp.exp(sc - m)
            den = jnp.sum(p, axis=1, keepdims=True)
            probs.append(p / den)
            lses.append(m + jnp.log(den))
        lmax = jnp.maximum(jnp.maximum(lses[0], lses[1]), lses[2])
        ws = [jnp.exp(l - lmax) for l in lses]
        wsum = ws[0] + ws[1] + ws[2]
        pw = (probs[0] * (ws[0] / wsum) + probs[1] * (ws[1] / wsum) + probs[2] * (ws[2] / wsum)).astype(BF16)
        o = lax.dot_general(pw[:, 0:WINDOW_MAX], cv_ref[hs, :].astype(BF16), NT_DIMS, preferred_element_type=F32)
        vn_pad = jnp.concatenate([vn_ref[:, hs], pad_new], axis=0).astype(BF16)
        attn_ref[:, hs] = o + _dot(pw[:, WINDOW_MAX:], vn_pad)


N_ATTN_SAMPLE_IN = 7
N_ATTN_SAMPLE_OUT = 3


def _attn_sample_hgrn_prompt_body(*refs, n_chunks, tiles_per_seq):
    n_in = N_ATTN_SAMPLE_IN + 4
    n_out = N_ATTN_SAMPLE_OUT + 2
    ins, outs, scratch = refs[:n_in], refs[n_in:n_in + n_out], refs[n_in + n_out:]
    o_ref, s_ref = outs[N_ATTN_SAMPLE_OUT:]
    st_ref = scratch[0]
    t = lax.rem(pl.program_id(0), tiles_per_seq)

    @pl.when(t == 0)
    def _():
        st_ref[...] = jnp.zeros_like(st_ref)

    _attn_sample_body(*ins[:N_ATTN_SAMPLE_IN], *outs[:N_ATTN_SAMPLE_OUT])
    _hgrn_prompt_tile(*ins[N_ATTN_SAMPLE_IN:], o_ref, *scratch, n_chunks=n_chunks)

    @pl.when(t == tiles_per_seq - 1)
    def _():
        for h in range(N_HEADS_B):
            s_ref[h] = st_ref[h].T


def _attn_sample_hgrn_prompt(q, kn, vn, cache_kt, cache_vt, base, masks, qb, fb, ib, lb_param, bsz, seq):
    dec_b = cache_kt.shape[0]
    tt = bsz * seq // dec_b
    assert bsz * seq == tt * dec_b and seq % tt == 0 and tt % GLA_CHUNK == 0
    n_chunks = tt // GLA_CHUNK
    small = pl.BlockSpec((DEC_SEQ, D_A), lambda i: (i, 0))
    big = pl.BlockSpec((None, D_A, WINDOW_MAX), lambda i: (i, 0, 0))
    const = lambda a: pl.BlockSpec(a.shape, lambda i: (0,) * a.ndim)
    blk = pl.BlockSpec((tt, D_F), lambda i: (i, 0))
    state_blk = pl.BlockSpec((None, N_HEADS_B, EXPAND_B, HEAD_V_B), lambda i: (i // (seq // tt), 0, 0, 0))
    tile = lambda dt: pltpu.VMEM((tt, D_F), dt)
    return pl.pallas_call(
        functools.partial(_attn_sample_hgrn_prompt_body, n_chunks=n_chunks, tiles_per_seq=seq // tt),
        grid=(dec_b,),
        in_specs=[small, small, small, big, big, const(base), const(masks), blk, blk, blk, const(lb_param)],
        out_specs=[small, big, big, blk, state_blk],
        out_shape=[jax.ShapeDtypeStruct((dec_b * DEC_SEQ, D_A), F32),
                   jax.ShapeDtypeStruct(cache_kt.shape, F32),
                   jax.ShapeDtypeStruct(cache_vt.shape, F32),
                   jax.ShapeDtypeStruct((bsz * seq, D_B), BF16),
                   jax.ShapeDtypeStruct((bsz, N_HEADS_B, EXPAND_B, HEAD_V_B), F32)],
        scratch_shapes=[pltpu.VMEM((N_HEADS_B, HEAD_V_B, EXPAND_B), F32),
                        tile(F32), tile(F32), tile(F32),
                        pltpu.VMEM((6, tt, D_F), F32),
                        tile(BF16), tile(BF16), tile(BF16),
                        pltpu.VMEM((n_chunks, N_HEADS_B, GLA_CHUNK, GLA_CHUNK), F32)],
        compiler_params=_cparams(("arbitrary",), 56),
        name="attn_sample_hgrn_prompt",
    )(q, kn, vn, cache_kt, cache_vt, base, masks, qb, fb, ib, lb_param)


def _merge_body(*refs, n_cfg):
    x_ref = refs[0]
    o_refs = refs[1:1 + n_cfg]
    l_refs = refs[1 + n_cfg:1 + 2 * n_cfg] if n_cfg > 1 else ()
    ga_ref, ob_ref, gb_ref, wout_ref, ng_ref, lg_ref, lbias_ref, y_ref = refs[1 + n_cfg + len(l_refs):]

    if n_cfg > 1:
        et_mat = _head_expand_matrix()
        ls =[r[...] for r in l_refs]
        lmax = functools.reduce(jnp.maximum, ls)
        ws = [jnp.exp(l - lmax) for l in ls]
        wsum = functools.reduce(lambda a, c: a + c, ws)
        attn = None
        for w, o_ref in zip(ws, o_refs):
            term = _dot2_rhs(w / wsum, et_mat) * o_ref[...].astype(F32)
            attn = term if attn is None else attn + term
    else:
        attn = o_refs[0][...].astype(F32)

    a_out = attn * _silu(ga_ref[...].astype(F32))
    parts = [a_out.astype(BF16)]
    for h in range(N_HEADS_B):
        hs = slice(h * HEAD_V_B, (h + 1) * HEAD_V_B)
        oh = ob_ref[:, hs].astype(F32)
        ms = jnp.mean(oh * oh, axis=-1, keepdims=True)
        on = oh * lax.rsqrt(ms + NORM_EPS) * ng_ref[...]
        parts.append((on * _silu(gb_ref[:, hs].astype(F32))).astype(BF16))
    mix = jnp.concatenate(parts, axis=-1)
    z = ALPHA * x_ref[...] + _dot(mix, wout_ref[...])
    mu = jnp.mean(z, axis=-1, keepdims=True)
    zc = z - mu
    var = jnp.mean(zc * zc, axis=-1, keepdims=True)
    y_ref[...] = zc * lax.rsqrt(var + NORM_EPS) * lg_ref[...] + lbias_ref[...]


def _merge(x2d, attn_outs, lses, ga, ob, gb, w_out_bf16, norm_g, ln_g, ln_b, tm):
    n = x2d.shape[0]
    n_cfg = len(attn_outs)
    row = lambda width: pl.BlockSpec((tm, width), lambda i: (i, 0))
    const = lambda shape: pl.BlockSpec(shape, lambda i: (0,) * len(shape))
    in_specs = ([row(D_MODEL)] + [row(D_A)] * n_cfg + [row(LANES)] * len(lses)
                + [row(D_A), row(D_B), row(D_B), const((D_MODEL, D_MODEL)),
                   const((1, HEAD_V_B)), const((1, D_MODEL)), const((1, D_MODEL))])
    return pl.pallas_call(
        functools.partial(_merge_body, n_cfg=n_cfg),
        grid=(n // tm,),
        in_specs=in_specs,
        out_specs=row(D_MODEL),
        out_shape=jax.ShapeDtypeStruct((n, D_MODEL), F32),
        compiler_params=_cparams(("arbitrary",), 48),
        name=f"merge_{n_cfg}",
    )(x2d, *attn_outs, *lses, ga, ob, gb, w_out_bf16, norm_g, ln_g, ln_b)


def _bias_by_distance(rel_bias, dist):
    onehot = jax.nn.one_hot(_rel_buckets(dist.reshape(-1)), NUM_BUCKETS, dtype=F32)
    table = jnp.dot(onehot, rel_bias.astype(F32), precision=lax.Precision.HIGHEST)
    return table.reshape(dist.shape + (N_HEADS_A,))


def _prompt_bias(rel_bias, dil):
    i = np.arange(ATTN_BLOCK)[:, None]
    j = np.arange(2 * ATTN_BLOCK)[None, :]
    sub = ATTN_BLOCK + i - j
    ok = (sub >= 0) & (sub <= WIN_STEPS)
    bias = _bias_by_distance(rel_bias, dil * np.clip(sub, 0, WIN_STEPS)) * LOG2_E
    full = jnp.where(ok[:, :, None], bias, -jnp.inf)
    no_prev = jnp.where((ok & (j >= ATTN_BLOCK))[:, :, None], bias, -jnp.inf)
    return jnp.transpose(jnp.concatenate([full, no_prev], axis=2), (2, 0, 1))


def _to_feature_major(cache):
    b, p, h, d = cache.shape
    return jnp.transpose(cache, (0, 2, 3, 1)).reshape(b, h * d, p)


def _from_feature_major(cache_t):
    b, _, p = cache_t.shape
    return jnp.transpose(cache_t.reshape(b, N_HEADS_A, HEAD_DIM_A, p), (0, 3, 1, 2))


def kernel(x_prompt, x_sample, cache_k, cache_v, state_hgrn, w_in, w_out, rel_bias, lb_param, hgrn_norm_g, ln_g, ln_b):
    bsz, seq, _ = x_prompt.shape
    dec_b, dec_t, _ = x_sample.shape
    assert dec_t == DEC_SEQ and cache_k.shape[2] == WINDOW_MAX and w_in.shape[0] == DEPTH
    keep = min(WINDOW_MAX, seq)

    w_in16 = w_in[0].astype(BF16)
    w_out16 = w_out[0].astype(BF16)
    norm_g = hgrn_norm_g[0].reshape(1, HEAD_V_B).astype(F32)
    lng = ln_g[0].reshape(1, D_MODEL).astype(F32)
    lnb = ln_b[0].reshape(1, D_MODEL).astype(F32)
    lbp = lb_param.astype(F32)

    tm = 512
    n_p = bsz * seq
    tiles_per_seq = seq // tm
    win_tiles = keep // tm
    main = lambda i: (i, 0)
    win = lambda i: (i // tiles_per_seq, 0, jnp.maximum(i % tiles_per_seq - (tiles_per_seq - win_tiles), 0))
    o_qa, o_ka, o_va, o_ga, o_qb, o_fb, o_ib, o_gb = (D_A * 0, D_A * 1, D_A * 2, D_A * 3, 4 * D_A,
                                                    4 * D_A + D_F, 4 * D_A + 2 * D_F, 4 * D_A + 2 * D_F + D_B)
    outs = [
        (o_qa, D_A, ATTN_SCALE * LOG2_E, BF16, n_p, main),
        (o_ka, D_A, 1.0, BF16, n_p, main),
        (o_va, D_A, 1.0, BF16, n_p, main),
        (o_ka, D_A, 1.0, F32, (bsz, keep), win),
        (o_va, D_A, 1.0, F32, (bsz, keep), win),
        (o_ga, D_A, 1.0, BF16, n_p, main),
        (o_qb, D_F, 1.0, F32, n_p, main),
        (o_fb, D_F, 1.0, F32, n_p, main),
        (o_ib, D_B, 1.0, F32, n_p, main),
        (o_gb, D_B, 1.0, BF16, n_p, main),
    ]
    x2d = x_prompt.reshape(n_p, D_MODEL)
    qa, ka, va, kwin, vwin, ga, qb, fb, ib, gb = _proj(x2d, w_in16, outs, tm)

    attn_outs, lses = [], []
    for dil in DILATIONS:
        o, lse = _attn_prompt(qa, ka, va, _prompt_bias(rel_bias, dil), dil, bsz, seq)
        attn_outs.append(o)
        lses.append(lse)

    n_s = dec_b * dec_t
    outs_s = [(lo, D_A, 1.0, F32, n_s, main) for lo in (o_qa, o_ka, o_va, o_ga, o_qb, o_fb, o_ib, o_gb)]
    xs2d = x_sample.reshape(n_s, D_MODEL)
    tm_s = min(tm, n_s)
    sqa, ska, sva, sga, sqb, sfb, sib, sgb = _proj(xs2d, w_in16, outs_s, tm_s)
    dist_s, masks_s = _sample_key_tables()
    base_s = jnp.transpose(_bias_by_distance(rel_bias, dist_s), (2, 0, 1))
    attn_s, k_new_t, v_new_t, ob, s_prompt = _attn_sample_hgrn_prompt(
        sqa, ska, sva, _to_feature_major(cache_k[0]), _to_feature_major(cache_v[0]), base_s, jnp.asarray(masks_s),
        qb, fb, ib, lbp, bsz, seq)
    y_prompt = _merge(x2d, attn_outs, lses, ga, ob, gb, w_out16, norm_g, lng, lnb, tm).reshape(bsz, seq, D_MODEL)
    ob_s, s_sample = _hgrn_sample(sqb, sfb, sib, lbp, state_hgrn[0], LANES // DEC_SEQ)
    y_sample = _merge(xs2d, [attn_s], [], sga, ob_s, sgb, w_out16, norm_g, lng, lnb, tm_s).reshape(dec_b, dec_t, D_MODEL)

    return (y_prompt, y_sample,
            _from_feature_major(kwin)[None], _from_feature_major(vwin)[None],
            _from_feature_major(k_new_t)[None], _from_feature_major(v_new_t)[None],
            s_prompt[None], s_sample[None].astype(state_hgrn.dtype))
```

```python
import functools

import numpy as np
import jax
import jax.numpy as jnp
from jax import lax
from jax.experimental import pallas as pl
from jax.experimental.pallas import tpu as pltpu

F32 = jnp.float32
BF16 = jnp.bfloat16

D_MODEL = 1024
D_A = 512
HEAD_DIM_A = 64
N_HEADS_A = 8
DILATIONS = (1, 4, 16)
WIN_STEPS = 128
WINDOW_MAX = 2048
ATTN_BLOCK = 128
ATTN_SCALE = HEAD_DIM_A ** -0.5
NUM_BUCKETS = 32
REL_MAX_DIST = 2048
D_B = 512
N_HEADS_B = 4
HEAD_V_B = 128
EXPAND_B = 128
D_F = 512
GLA_CHUNK = 64
D_IN = 4 * D_A + 2 * D_F + 2 * D_B
DEPTH = 1
ALPHA = (2.0 * DEPTH) ** 0.25
NORM_EPS = 1e-5
DEC_SEQ = 8
LOG2_E = 1.4426950408889634
LN_2 = 0.6931471805599453

LANES = 128
MIB = 1024 * 1024

NT_DIMS = (((1,), (1,)), ((), ()))
TN_DIMS = (((0,), (0,)), ((), ()))


def _cparams(semantics, vmem_mib):
    return pltpu.CompilerParams(dimension_semantics=semantics, vmem_limit_bytes=vmem_mib * MIB)


def _rel_buckets(dist):
    max_exact = NUM_BUCKETS // 2
    d = np.maximum(dist, 1).astype(np.float32)
    large = max_exact + (np.log(d / max_exact) / np.log(REL_MAX_DIST / max_exact)
                         * (NUM_BUCKETS - max_exact)).astype(np.int32)
    large = np.minimum(large, NUM_BUCKETS - 1)
    return np.where(dist < max_exact, dist, large).astype(np.int32)


def _split3(x):
    hi = x.astype(BF16)
    r1 = x - hi.astype(F32)
    mid = r1.astype(BF16)
    lo = (r1 - mid.astype(F32)).astype(BF16)
    return hi, mid, lo


def _dot(a, b):
    return jnp.dot(a, b, preferred_element_type=F32)


def _dot3(mat_bf16, x):
    hi, mid, lo = _split3(x)
    return _dot(mat_bf16, hi) + _dot(mat_bf16, mid) + _dot(mat_bf16, lo)


def _split2(x):
    hi = x.astype(BF16)
    return hi, (x - hi.astype(F32)).astype(BF16)


def _dot2_rhs(x, mat_bf16):
    hi, lo = _split2(x)
    return _dot(hi, mat_bf16) + _dot(lo, mat_bf16)


def _dot2_lhs(mat_bf16, x):
    hi, lo = _split2(x)
    return _dot(mat_bf16, hi) + _dot(mat_bf16, lo)


def _sigmoid(x):
    return 0.5 * jnp.tanh(0.5 * x) + 0.5


def _silu(x):
    return x * _sigmoid(x)


def _bcast_rows(x, block, j):
    rows = x.shape[0]
    parts = [jnp.broadcast_to(x[r0 + j:r0 + j + 1, :], (block, x.shape[1])) for r0 in range(0, rows, block)]
    return parts[0] if len(parts) == 1 else jnp.concatenate(parts, axis=0)


def _proj_body(x_ref, w_ref, *out_refs, groups):
    x = x_ref[...].astype(BF16)
    for (lo, width, scale, idxs) in groups:
        h = _dot(x, w_ref[:, lo:lo + width])
        if scale != 1.0:
            h = h * scale
        for i, transposed in idxs:
            out_refs[i][...] = (h.T if transposed else h).astype(out_refs[i].dtype)


def _proj(x2d, w_bf16, outs, tm):
    n = x2d.shape[0]
    groups = {}
    for i, (lo, width, scale, _, rows, _) in enumerate(outs):
        groups.setdefault((lo, width, scale), []).append((i, isinstance(rows, tuple)))
    groups = tuple((lo, width, scale, tuple(idxs)) for (lo, width, scale), idxs in groups.items())
    out_shape, out_specs = [], []
    for (_, width, _, dt, rows, imap) in outs:
        if isinstance(rows, tuple):
            out_shape.append(jax.ShapeDtypeStruct((rows[0], width, rows[1]), dt))
            out_specs.append(pl.BlockSpec((None, width, tm), imap))
        else:
            out_shape.append(jax.ShapeDtypeStruct((rows, width), dt))
            out_specs.append(pl.BlockSpec((tm, width), imap))
    return pl.pallas_call(
        functools.partial(_proj_body, groups=groups),
        grid=(n // tm,),
        in_specs=[pl.BlockSpec((tm, D_MODEL), lambda i: (i, 0)),
                  pl.BlockSpec((D_MODEL, D_IN), lambda i: (0, 0))],
        out_specs=out_specs,
        out_shape=out_shape,
        compiler_params=_cparams(("arbitrary",), 56),
        name="proj",
    )(x2d, w_bf16)


PROJ_TILE = 512
N_PAIRS = D_A // LANES


def _blocked_shape(bsz, seq, dil, width, dtype):
    return jax.ShapeDtypeStruct((bsz, dil, seq // (dil * ATTN_BLOCK), ATTN_BLOCK, width), dtype)


def _blocked_tile_spec(seq, dil, width):
    tiles = seq // PROJ_TILE
    rows = PROJ_TILE // dil
    if rows >= ATTN_BLOCK:
        return pl.BlockSpec((None, dil, rows // ATTN_BLOCK, ATTN_BLOCK, width),
                            lambda i: (i // tiles, 0, i % tiles, 0, 0))
    per = ATTN_BLOCK // rows
    return pl.BlockSpec((None, dil, None, rows, width),
                        lambda i: (i // tiles, 0, (i % tiles) // per, (i % tiles) % per, 0))


def _store_blocked(out_refs, slabs, slabs4):
    assert DILATIONS == (1, 4, 16) and PROJ_TILE == 4 * ATTN_BLOCK
    out1, out4, out16 = out_refs
    sub = ATTN_BLOCK // 4
    for j in range(slabs.shape[0]):
        lanes = slice(LANES * j, LANES * (j + 1))
        for k in range(4):
            rows = slice(k * ATTN_BLOCK, (k + 1) * ATTN_BLOCK)
            out1[0, k, :, lanes] = slabs[j, rows, :].astype(out1.dtype)
        for c in range(4):
            piece = slabs[j, pl.ds(c, ATTN_BLOCK, stride=4), :]
            slabs4[j, c * ATTN_BLOCK:(c + 1) * ATTN_BLOCK, :] = piece
            out4[c, 0, :, lanes] = piece.astype(out4.dtype)
        for c in range(4):
            for r in range(4):
                piece = slabs4[j, pl.ds(c * ATTN_BLOCK + r, sub, stride=4), :]
                out16[c + 4 * r, :, lanes] = piece.astype(out16.dtype)


def _load_blocked(in_ref, slabs, dil, slabs4=None):
    sub = ATTN_BLOCK // 4
    for j in range(slabs.shape[0]):
        lanes = slice(LANES * j, LANES * (j + 1))
        for c in range(4):
            if dil == 4:
                piece = in_ref[c, 0, :, lanes].astype(F32)
            else:
                for r in range(4):
                    slabs4[j, pl.ds(c * ATTN_BLOCK + r, sub, stride=4), :] = in_ref[c + 4 * r, :, lanes].astype(F32)
                piece = slabs4[j, c * ATTN_BLOCK:(c + 1) * ATTN_BLOCK, :]
            slabs[j, pl.ds(c, ATTN_BLOCK, stride=4), :] = piece


def _proj_prompt_body(x_ref, w_ref, *refs):
    n_cfg = len(DILATIONS)
    blocked = [refs[n_cfg * g:n_cfg * (g + 1)] for g in range(3)]
    kwin_ref, vwin_ref, ga_ref, qb_ref, fb_ref, ib_ref, gb_ref = refs[3 * n_cfg:3 * n_cfg + 7]
    x16_ref = refs[3 * n_cfg + 7]
    slab_sets = refs[3 * n_cfg + 8:3 * n_cfg + 11]
    slab4_sets = refs[3 * n_cfg + 11:]
    x16_ref[...] = x_ref[...].astype(BF16)
    for g, scale in enumerate((ATTN_SCALE * LOG2_E, 1.0, 1.0)):
        h = _dot(x16_ref[...], w_ref[:, D_A * g:D_A * (g + 1)])
        for j in range(N_PAIRS):
            piece = h[:, LANES * j:LANES * (j + 1)]
            slab_sets[g][j] = piece * scale if scale != 1.0 else piece
    lo = 3 * D_A
    for ref, width in ((ga_ref, D_A), (qb_ref, D_F), (fb_ref, D_F), (ib_ref, D_B), (gb_ref, D_B)):
        ref[...] = _dot(x16_ref[...], w_ref[:, lo:lo + width]).astype(ref.dtype)
        lo += width
    for g, win_ref in enumerate((None, kwin_ref, vwin_ref)):
        if win_ref is not None:
            for j in range(N_PAIRS):
                win_ref[LANES * j:LANES * (j + 1), :] = slab_sets[g][j].T
        _store_blocked(blocked[g], slab_sets[g], slab4_sets[g])


def _proj_prompt(x2d, w_bf16, bsz, seq, keep):
    n = bsz * seq
    tiles = seq // PROJ_TILE
    win_tiles = keep // PROJ_TILE
    row = lambda width: pl.BlockSpec((PROJ_TILE, width), lambda i: (i, 0))
    win = pl.BlockSpec((None, D_A, PROJ_TILE),
                       lambda i: (i // tiles, 0, jnp.maximum(i % tiles - (tiles - win_tiles), 0)))
    out_shape, out_specs = [], []
    for _ in range(3):
        for dil in DILATIONS:
            out_shape.append(_blocked_shape(bsz, seq, dil, D_A, BF16))
            out_specs.append(_blocked_tile_spec(seq, dil, D_A))
    for _ in range(2):
        out_shape.append(jax.ShapeDtypeStruct((bsz, D_A, keep), F32))
        out_specs.append(win)
    for width, dt in ((D_A, BF16), (D_F, F32), (D_F, F32), (D_B, F32), (D_B, BF16)):
        out_shape.append(jax.ShapeDtypeStruct((n, width), dt))
        out_specs.append(row(width))
    return pl.pallas_call(
        _proj_prompt_body,
        grid=(n // PROJ_TILE,),
        in_specs=[row(D_MODEL), pl.BlockSpec((D_MODEL, D_IN), lambda i: (0, 0))],
        out_specs=out_specs,
        out_shape=out_shape,
        scratch_shapes=([pltpu.VMEM((PROJ_TILE, D_MODEL), BF16)]
                        + [pltpu.VMEM((N_PAIRS, PROJ_TILE, LANES), F32) for _ in range(6)]),
        compiler_params=_cparams(("arbitrary",), 56),
        name="proj_prompt",
    )(x2d, w_bf16)


UNITS_PER_STEP = 16


def _attn_unit(q, k_prev, k_cur, v_prev, v_cur, bias_ref, bias_set):
    lane = _iota2((ATTN_BLOCK, LANES), 1)
    low_half = lane < HEAD_DIM_A
    scores = []
    for j in range(N_PAIRS):
        lanes = slice(LANES * j, LANES * (j + 1))
        qj = q[:, lanes]
        zero = jnp.zeros_like(qj)
        kcat = jnp.concatenate([k_prev[:, lanes], k_cur[:, lanes]], axis=0)
        for qh in (jnp.where(low_half, qj, zero), jnp.where(low_half, zero, qj)):
            scores.append(lax.dot_general(qh, kcat, NT_DIMS, preferred_element_type=F32))

    probs, inv_dens = [], []
    lse_tile = jnp.zeros((ATTN_BLOCK, LANES), F32)
    for h in range(N_HEADS_A):
        s = scores[h] + bias_ref[bias_set + h]
        m = jnp.max(jnp.maximum(s[:, :ATTN_BLOCK], s[:, ATTN_BLOCK:]), axis=1, keepdims=True)
        p = jnp.exp2(s - m)
        den = jnp.sum(p[:, :ATTN_BLOCK] + p[:, ATTN_BLOCK:], axis=1, keepdims=True)
        probs.append(p.astype(BF16))
        inv_dens.append(1.0 / den)
        lse_tile = jnp.where(lane == h, (m + jnp.log2(den)) * LN_2, lse_tile)

    outs = []
    for j in range(N_PAIRS):
        lanes = slice(LANES * j, LANES * (j + 1))
        vcat = jnp.concatenate([v_prev[:, lanes], v_cur[:, lanes]], axis=0)
        o_a = _dot(probs[2 * j], vcat) * inv_dens[2 * j]
        o_b = _dot(probs[2 * j + 1], vcat) * inv_dens[2 * j + 1]
        outs.append(jnp.where(low_half, o_a, o_b))
    return outs, lse_tile


def _attn_prompt_body(q_ref, k_ref, v_ref, bias_ref, o_ref, lse_ref, kc_ref, vc_ref, *, n_res, n_blk):
    step = pl.program_id(2)

    @pl.when(step == 0)
    def _():
        kc_ref[...] = jnp.zeros_like(kc_ref)
        vc_ref[...] = jnp.zeros_like(vc_ref)

    def run(rr, n, k_prev, v_prev, bias_set):
        outs, lse_tile = _attn_unit(q_ref[rr, n], k_prev, k_ref[rr, n], v_prev, v_ref[rr, n], bias_ref, bias_set)
        for j in range(N_PAIRS):
            o_ref[rr, n, :, LANES * j:LANES * (j + 1)] = outs[j].astype(o_ref.dtype)
        lse_ref[rr, n] = lse_tile

    start_set = jnp.where(step == 0, N_HEADS_A, 0)
    for rr in range(n_res):
        run(rr, 0, kc_ref[rr], vc_ref[rr], start_set)

        def later(n, carry, rr=rr):
            run(rr, n, k_ref[rr, n - 1], v_ref[rr, n - 1], 0)
            return carry

        lax.fori_loop(1, n_blk, later, 0, unroll=3)
        kc_ref[rr] = k_ref[rr, n_blk - 1]
        vc_ref[rr] = v_ref[rr, n_blk - 1]


def _attn_prompt(q, k, v, bias, dil):
    b, _, nb, _, _ = q.shape
    n_blk = min(nb, UNITS_PER_STEP)
    n_res = min(dil, UNITS_PER_STEP // n_blk)
    blk = lambda width: pl.BlockSpec((None, n_res, n_blk, ATTN_BLOCK, width), lambda bi, ri, si: (bi, ri, si, 0, 0))
    return pl.pallas_call(
        functools.partial(_attn_prompt_body, n_res=n_res, n_blk=n_blk),
        grid=(b, dil // n_res, nb // n_blk),
        in_specs=[blk(D_A), blk(D_A), blk(D_A),
                  pl.BlockSpec((2 * N_HEADS_A, ATTN_BLOCK, 2 * ATTN_BLOCK), lambda bi, ri, si: (0, 0, 0))],
        out_specs=[blk(D_A), blk(LANES)],
        out_shape=[jax.ShapeDtypeStruct(q.shape, BF16),
                   jax.ShapeDtypeStruct(q.shape[:-1] + (LANES,), F32)],
        scratch_shapes=[pltpu.VMEM((n_res, ATTN_BLOCK, D_A), BF16), pltpu.VMEM((n_res, ATTN_BLOCK, D_A), BF16)],
        compiler_params=_cparams(("arbitrary", "arbitrary", "arbitrary"), 40),
        name=f"attn_prompt_d{dil}",
    )(q, k, v, bias)


def _lower_bound(lbp_ref):
    p0 = lbp_ref[0:1, :]
    p1 = lbp_ref[1:2, :]
    m = jnp.maximum(p0, p1)
    e0 = jnp.exp(p0 - m)
    e1 = jnp.exp(p1 - m)
    return e0 / (e0 + e1)


def _gates(qb, fb, lb):
    f = lb + (1.0 - lb) * _sigmoid(fb)
    return _silu(qb) * (EXPAND_B ** -0.5), 1.0 - f, jnp.log(f)


def _iota2(shape, dim):
    return lax.broadcasted_iota(jnp.int32, shape, dim)


def _div2(x, n):
    return jnp.right_shift(x, int(n).bit_length() - 1)


def _mod2(x, n):
    return jnp.bitwise_and(x, n - 1)


def _head_expand_matrix():
    return (_iota2((LANES, D_A), 0) == _div2(_iota2((LANES, D_A), 1), HEAD_DIM_A)).astype(BF16)


def _diag_block_scores(q, kk, b, a_heads, row, col, block):
    same = _div2(row, block) == _div2(col, block)
    for j in range(block):
        bj = _bcast_rows(b, block, j)
        kj = _bcast_rows(kk, block, j)
        x = q * jnp.exp(b - bj) * kj
        sel = same & (_mod2(col, block) == j) & (_mod2(row, block) >= j)
        for h in range(N_HEADS_B):
            cs = jnp.sum(x[:, h * EXPAND_B:(h + 1) * EXPAND_B], axis=-1, keepdims=True)
            a_heads[h] = jnp.where(sel, cs, a_heads[h])
    return a_heads


def _hgrn_prompt_tile(qb_ref, fb_ref, ib_ref, lbp_ref, o_ref, st_ref, q_s, kk_s, b_s, ref_s, qin_s, kout_s, v_s, a_s,
                      *, n_chunks):
    c_len = GLA_CHUNK
    chunks = [slice(c * c_len, (c + 1) * c_len) for c in range(n_chunks)]
    heads = [slice(h * EXPAND_B, (h + 1) * EXPAND_B) for h in range(N_HEADS_B)]

    lb = _lower_bound(lbp_ref)
    row = _iota2((c_len, c_len), 0)
    col = _iota2((c_len, c_len), 1)
    tri = (col <= row).astype(BF16)
    half_sizes = (32, 16, 8, 4, 2, 1)
    masks, picks = [], []
    for n in half_sizes:
        masks.append((_div2(row, 2 * n) == _div2(col, 2 * n)) & (_mod2(_div2(row, n), 2) == 1)
                     & (_mod2(_div2(col, n), 2) == 0))
        picks.append((col == _div2(row, 2 * n) * (2 * n) + (n - 1)).astype(BF16))
    pick_all = jnp.concatenate(picks, axis=0)
    on_diag = row == col

    for c, cs in enumerate(chunks):
        q, kk, g = _gates(qb_ref[cs, :], fb_ref[cs, :], lb)
        q_s[cs, :] = q
        kk_s[cs, :] = kk
        v_s[cs, :] = ib_ref[cs, :].astype(BF16)
        b_c = _dot2_lhs(tri, g)
        b_s[cs, :] = b_c
        refs = _dot(pick_all, b_c.astype(BF16))
        for li in range(len(half_sizes)):
            ref_s[li, cs, :] = refs[li * c_len:(li + 1) * c_len, :]
        b_end = b_c[c_len - 1:c_len, :]
        qin_s[cs, :] = (q * jnp.exp(b_c)).astype(BF16)
        kout_s[cs, :] = (kk * jnp.exp(b_end - b_c)).astype(BF16)
        qk = q * kk
        for h, hs in enumerate(heads):
            a_s[c, h] = jnp.where(on_diag, jnp.sum(qk[:, hs], axis=-1, keepdims=True), 0.0)
    for li, mask in enumerate(masks):
        for c, cs in enumerate(chunks):
            d = b_s[cs, :] - ref_s[li, cs, :]
            ql = (q_s[cs, :] * jnp.exp(d)).astype(BF16)
            kl = (kk_s[cs, :] * jnp.exp(-d)).astype(BF16)
            for h, hs in enumerate(heads):
                part = lax.dot_general(ql[:, hs], kl[:, hs], NT_DIMS, preferred_element_type=F32)
                a_s[c, h] = jnp.where(mask, part, a_s[c, h])

    for c, cs in enumerate(chunks):
        dec = jnp.exp(b_s[c * c_len + c_len - 1:(c + 1) * c_len, :])
        for h, hs in enumerate(heads):
            state = st_ref[h]
            inter = lax.dot_general(qin_s[cs, hs], state.astype(BF16), NT_DIMS, preferred_element_type=F32)
            intra = _dot(a_s[c, h].astype(BF16), v_s[cs, hs])
            o_ref[cs, hs] = (inter + intra).astype(o_ref.dtype)
            upd = lax.dot_general(v_s[cs, hs], kout_s[cs, hs], TN_DIMS, preferred_element_type=F32)
            st_ref[h] = state * dec[:, hs] + upd


def _hgrn_sample_body(qb_ref, fb_ref, ib_ref, lbp_ref, s0_ref, o_ref, s_ref, *, nbatch):
    rows_n = nbatch * DEC_SEQ
    assert rows_n == LANES
    lb = _lower_bound(lbp_ref)
    row = _iota2((rows_n, rows_n), 0)
    col = _iota2((rows_n, rows_n), 1)
    same = _div2(row, DEC_SEQ) == _div2(col, DEC_SEQ)
    tri = (same & (col <= row)).astype(BF16)
    pick = (col == row * DEC_SEQ + (DEC_SEQ - 1)).astype(BF16)

    q, kk, g = _gates(qb_ref[...], fb_ref[...], lb)
    v = ib_ref[...]
    b = _dot3(tri, g)
    b_end = _bcast_rows(b, DEC_SEQ, DEC_SEQ - 1)
    q_in = (q * jnp.exp(b)).astype(BF16)
    k_out = kk * jnp.exp(b_end - b)
    v16 = v.astype(BF16)

    a_heads = [jnp.zeros((rows_n, rows_n), F32) for _ in range(N_HEADS_B)]
    a_heads = _diag_block_scores(q, kk, b, a_heads, row, col, DEC_SEQ)

    rsel = _div2(_iota2((rows_n, 1), 0), DEC_SEQ)
    for h in range(N_HEADS_B):
        hs = slice(h * EXPAND_B, (h + 1) * EXPAND_B)
        intra = _dot(a_heads[h].astype(BF16), v16[:, hs])
        dec_t = jnp.exp(_dot3(pick, b[:, hs])).T
        k_out_t = k_out[:, hs].T.astype(BF16)
        for bi in range(nbatch):
            rs = slice(bi * DEC_SEQ, (bi + 1) * DEC_SEQ)
            s0 = s0_ref[bi, h]
            inter = _dot(q_in[rs, hs], s0.astype(BF16))
            o_ref[rs, hs] = inter + intra[rs, :]
            v_b = jnp.where(rsel == bi, v16[:, hs], jnp.zeros_like(v16[:, hs]))
            upd = _dot(k_out_t, v_b)
            s_ref[bi, h] = s0 * dec_t[:, bi:bi + 1] + upd


def _hgrn_sample(qb, fb, ib, lb_param, s0, nbatch):
    b = s0.shape[0]
    rows_n = nbatch * DEC_SEQ
    blk = pl.BlockSpec((rows_n, D_F), lambda i: (i, 0))
    sblk = pl.BlockSpec((nbatch, N_HEADS_B, EXPAND_B, HEAD_V_B), lambda i: (i, 0, 0, 0))
    return pl.pallas_call(
        functools.partial(_hgrn_sample_body, nbatch=nbatch),
        grid=(b // nbatch,),
        in_specs=[blk, blk, blk, pl.BlockSpec((DEPTH + 1, D_F), lambda i: (0, 0)), sblk],
        out_specs=[blk, sblk],
        out_shape=[jax.ShapeDtypeStruct((b * DEC_SEQ, D_B), F32),
                   jax.ShapeDtypeStruct(s0.shape, F32)],
        compiler_params=_cparams(("arbitrary",), 32),
        name="hgrn_sample",
    )(qb, fb, ib, lb_param, s0)


EXT_KEYS = WINDOW_MAX + LANES
SHIFT_ROWS = 64


def _sample_key_tables():
    t = np.arange(DEC_SEQ)[:, None]
    e = np.arange(EXT_KEYS)[None, :]
    is_new = e >= WINDOW_MAX
    d = np.where(is_new, t - (e - WINDOW_MAX), WINDOW_MAX + t - e)
    masks = np.full((len(DILATIONS), DEC_SEQ, EXT_KEYS), -np.inf, np.float32)
    for ci, dil in enumerate(DILATIONS):
        ok = (d >= 0) & (d % dil == 0) & (d <= WIN_STEPS * dil) & (e < WINDOW_MAX + DEC_SEQ)
        masks[ci][ok] = 0.0
    return np.maximum(d, 0), masks


def _attn_sample_body(q_ref, kn_ref, vn_ref, ck_ref, cv_ref, base_ref, mask_ref, attn_ref, ok_ref, ov_ref):
    keep = LANES - DEC_SEQ
    lane = _iota2((SHIFT_ROWS, LANES), 1)
    pad_rows = jnp.zeros((keep, D_A), F32)
    for c_ref, n_ref, o_ref in ((ck_ref, kn_ref, ok_ref), (cv_ref, vn_ref, ov_ref)):
        new_t = jnp.concatenate([pad_rows, n_ref[...]], axis=0).T
        for r0 in range(0, D_A, SHIFT_ROWS):
            rows = slice(r0, r0 + SHIFT_ROWS)
            rolled = pltpu.roll(c_ref[rows, :], WINDOW_MAX - DEC_SEQ, axis=1)
            o_ref[rows, 0:WINDOW_MAX - LANES] = rolled[:, 0:WINDOW_MAX - LANES]
            o_ref[rows, WINDOW_MAX - LANES:] = jnp.where(lane >= keep, new_t[rows, :], rolled[:, WINDOW_MAX - LANES:])

    qs = q_ref[...] * ATTN_SCALE
    pad_new = jnp.zeros((keep, HEAD_DIM_A), F32)
    for h in range(N_HEADS_A):
        hs = slice(HEAD_DIM_A * h, HEAD_DIM_A * (h + 1))
        qh = qs[:, hs].astype(BF16)
        s_cache = _dot(qh, ck_ref[hs, :].astype(BF16))
        kn_pad = jnp.concatenate([kn_ref[:, hs], pad_new], axis=0).astype(BF16)
        s_new = lax.dot_general(qh, kn_pad, NT_DIMS, preferred_element_type=F32)
        s = jnp.concatenate([s_cache, s_new], axis=1) + base_ref[h]
        probs, lses = [], []
        for ci in range(len(DILATIONS)):
            sc = s + mask_ref[ci]
            m = jnp.max(sc, axis=1, keepdims=True)
            p = jnp.exp(sc - m)
            den = jnp.sum(p, axis=1, keepdims=True)
            probs.append(p / den)
            lses.append(m + jnp.log(den))
        lmax = jnp.maximum(jnp.maximum(lses[0], lses[1]), lses[2])
        ws = [jnp.exp(l - lmax) for l in lses]
        wsum = ws[0] + ws[1] + ws[2]
        pw = (probs[0] * (ws[0] / wsum) + probs[1] * (ws[1] / wsum) + probs[2] * (ws[2] / wsum)).astype(BF16)
        o = lax.dot_general(pw[:, 0:WINDOW_MAX], cv_ref[hs, :].astype(BF16), NT_DIMS, preferred_element_type=F32)
        vn_pad = jnp.concatenate([vn_ref[:, hs], pad_new], axis=0).astype(BF16)
        attn_ref[:, hs] = o + _dot(pw[:, WINDOW_MAX:], vn_pad)


N_ATTN_SAMPLE_IN = 7
N_ATTN_SAMPLE_OUT = 3


def _attn_sample_hgrn_prompt_body(*refs, n_chunks, tiles_per_seq):
    n_in = N_ATTN_SAMPLE_IN + 4
    n_out = N_ATTN_SAMPLE_OUT + 2
    ins, outs, scratch = refs[:n_in], refs[n_in:n_in + n_out], refs[n_in + n_out:]
    o_ref, s_ref = outs[N_ATTN_SAMPLE_OUT:]
    st_ref = scratch[0]
    t = lax.rem(pl.program_id(0), tiles_per_seq)

    @pl.when(t == 0)
    def _():
        st_ref[...] = jnp.zeros_like(st_ref)

    _attn_sample_body(*ins[:N_ATTN_SAMPLE_IN], *outs[:N_ATTN_SAMPLE_OUT])
    _hgrn_prompt_tile(*ins[N_ATTN_SAMPLE_IN:], o_ref, *scratch, n_chunks=n_chunks)

    @pl.when(t == tiles_per_seq - 1)
    def _():
        for h in range(N_HEADS_B):
            s_ref[h] = st_ref[h].T


def _attn_sample_hgrn_prompt(q, kn, vn, cache_kt, cache_vt, base, masks, qb, fb, ib, lb_param, bsz, seq):
    dec_b = cache_kt.shape[0]
    tt = bsz * seq // dec_b
    assert bsz * seq == tt * dec_b and seq % tt == 0 and tt % GLA_CHUNK == 0
    n_chunks = tt // GLA_CHUNK
    small = pl.BlockSpec((DEC_SEQ, D_A), lambda i: (i, 0))
    big = pl.BlockSpec((None, D_A, WINDOW_MAX), lambda i: (i, 0, 0))
    const = lambda a: pl.BlockSpec(a.shape, lambda i: (0,) * a.ndim)
    blk = pl.BlockSpec((tt, D_F), lambda i: (i, 0))
    state_blk = pl.BlockSpec((None, N_HEADS_B, EXPAND_B, HEAD_V_B), lambda i: (i // (seq // tt), 0, 0, 0))
    tile = lambda dt: pltpu.VMEM((tt, D_F), dt)
    return pl.pallas_call(
        functools.partial(_attn_sample_hgrn_prompt_body, n_chunks=n_chunks, tiles_per_seq=seq // tt),
        grid=(dec_b,),
        in_specs=[small, small, small, big, big, const(base), const(masks), blk, blk, blk, const(lb_param)],
        out_specs=[small, big, big, blk, state_blk],
        out_shape=[jax.ShapeDtypeStruct((dec_b * DEC_SEQ, D_A), F32),
                   jax.ShapeDtypeStruct(cache_kt.shape, F32),
                   jax.ShapeDtypeStruct(cache_vt.shape, F32),
                   jax.ShapeDtypeStruct((bsz * seq, D_B), BF16),
                   jax.ShapeDtypeStruct((bsz, N_HEADS_B, EXPAND_B, HEAD_V_B), F32)],
        scratch_shapes=[pltpu.VMEM((N_HEADS_B, HEAD_V_B, EXPAND_B), F32),
                        tile(F32), tile(F32), tile(F32),
                        pltpu.VMEM((6, tt, D_F), F32),
                        tile(BF16), tile(BF16), tile(BF16),
                        pltpu.VMEM((n_chunks, N_HEADS_B, GLA_CHUNK, GLA_CHUNK), F32)],
        compiler_params=_cparams(("arbitrary",), 56),
        name="attn_sample_hgrn_prompt",
    )(q, kn, vn, cache_kt, cache_vt, base, masks, qb, fb, ib, lb_param)


def _merge_body(*refs, n_cfg):
    x_ref = refs[0]
    pairs = [slice(LANES * j, LANES * (j + 1)) for j in range(N_PAIRS)]
    if n_cfg > 1:
        o_refs = refs[1:1 + n_cfg]
        l_refs = refs[1 + n_cfg:1 + 2 * n_cfg]
        tail = refs[1 + 2 * n_cfg:9 + 2 * n_cfg]
        scratch = refs[9 + 2 * n_cfg:]
        o_nat, l_nat = [], []
        for ci, dil in enumerate(DILATIONS):
            if dil == 1:
                blocks = range(PROJ_TILE // ATTN_BLOCK)
                o_val = jnp.concatenate([o_refs[ci][0, k] for k in blocks], axis=0).astype(F32)
                o_nat.append([o_val[:, lanes] for lanes in pairs])
                l_nat.append(jnp.concatenate([l_refs[ci][0, k] for k in blocks], axis=0))
            else:
                o_slabs, l_slab = scratch[2 * (ci - 1)], scratch[2 * (ci - 1) + 1]
                o_tmp, l_tmp = scratch[-2:]
                _load_blocked(o_refs[ci], o_slabs, dil, o_tmp)
                _load_blocked(l_refs[ci], l_slab, dil, l_tmp)
                o_nat.append([o_slabs[j] for j in range(N_PAIRS)])
                l_nat.append(l_slab[0])
        et_mat = _head_expand_matrix()
        lmax = functools.reduce(jnp.maximum, l_nat)
        ws = [jnp.exp(l - lmax) for l in l_nat]
        wsum = functools.reduce(lambda a, c: a + c, ws)
        attn = [None] * N_PAIRS
        for w, o_c in zip(ws, o_nat):
            w_lanes = _dot2_rhs(w / wsum, et_mat)
            for j, lanes in enumerate(pairs):
                term = w_lanes[:, lanes] * o_c[j]
                attn[j] = term if attn[j] is None else attn[j] + term
    else:
        tail = refs[2:10]
        attn = [refs[1][:, lanes].astype(F32) for lanes in pairs]
    ga_ref, ob_ref, gb_ref, wout_ref, ng_ref, lg_ref, lbias_ref, y_ref = tail

    parts = [(attn[j] * _silu(ga_ref[:, lanes].astype(F32))).astype(BF16) for j, lanes in enumerate(pairs)]
    for h in range(N_HEADS_B):
        hs = slice(h * HEAD_V_B, (h + 1) * HEAD_V_B)
        oh = ob_ref[:, hs].astype(F32)
        ms = jnp.mean(oh * oh, axis=-1, keepdims=True)
        on = oh * lax.rsqrt(ms + NORM_EPS) * ng_ref[...]
        parts.append((on * _silu(gb_ref[:, hs].astype(F32))).astype(BF16))
    mix = jnp.concatenate(parts, axis=-1)
    z = ALPHA * x_ref[...] + _dot(mix, wout_ref[...])
    mu = jnp.mean(z, axis=-1, keepdims=True)
    zc = z - mu
    var = jnp.mean(zc * zc, axis=-1, keepdims=True)
    y_ref[...] = zc * lax.rsqrt(var + NORM_EPS) * lg_ref[...] + lbias_ref[...]


def _merge(x2d, attn_outs, lses, ga, ob, gb, w_out_bf16, norm_g, ln_g, ln_b, tm, seq=None):
    n = x2d.shape[0]
    n_cfg = len(attn_outs)
    row = lambda width: pl.BlockSpec((tm, width), lambda i: (i, 0))
    const = lambda shape: pl.BlockSpec(shape, lambda i: (0,) * len(shape))
    if n_cfg > 1:
        assert tm == PROJ_TILE and n_cfg == len(DILATIONS)
        attn_specs = ([_blocked_tile_spec(seq, dil, D_A) for dil in DILATIONS]
                      + [_blocked_tile_spec(seq, dil, LANES) for dil in DILATIONS])
        scratch = []
        for dil in DILATIONS[1:] + (0,):
            scratch += [pltpu.VMEM((N_PAIRS, tm, LANES), F32), pltpu.VMEM((1, tm, LANES), F32)]
    else:
        attn_specs, scratch = [row(D_A)], []
    in_specs = ([row(D_MODEL)] + attn_specs
                + [row(D_A), row(D_B), row(D_B), const((D_MODEL, D_MODEL)),
                   const((1, HEAD_V_B)), const((1, D_MODEL)), const((1, D_MODEL))])
    return pl.pallas_call(
        functools.partial(_merge_body, n_cfg=n_cfg),
        grid=(n // tm,),
        in_specs=in_specs,
        out_specs=row(D_MODEL),
        out_shape=jax.ShapeDtypeStruct((n, D_MODEL), F32),
        scratch_shapes=scratch,
        compiler_params=_cparams(("arbitrary",), 48),
        name=f"merge_{n_cfg}",
    )(x2d, *attn_outs, *lses, ga, ob, gb, w_out_bf16, norm_g, ln_g, ln_b)


def _bias_by_distance(rel_bias, dist):
    onehot = jax.nn.one_hot(_rel_buckets(dist.reshape(-1)), NUM_BUCKETS, dtype=F32)
    table = jnp.dot(onehot, rel_bias.astype(F32), precision=lax.Precision.HIGHEST)
    return table.reshape(dist.shape + (N_HEADS_A,))


def _prompt_bias(rel_bias, dil):
    i = np.arange(ATTN_BLOCK)[:, None]
    j = np.arange(2 * ATTN_BLOCK)[None, :]
    sub = ATTN_BLOCK + i - j
    ok = (sub >= 0) & (sub <= WIN_STEPS)
    bias = _bias_by_distance(rel_bias, dil * np.clip(sub, 0, WIN_STEPS)) * LOG2_E
    full = jnp.where(ok[:, :, None], bias, -jnp.inf)
    no_prev = jnp.where((ok & (j >= ATTN_BLOCK))[:, :, None], bias, -jnp.inf)
    return jnp.transpose(jnp.concatenate([full, no_prev], axis=2), (2, 0, 1))


def _to_feature_major(cache):
    b, p, h, d = cache.shape
    return jnp.transpose(cache, (0, 2, 3, 1)).reshape(b, h * d, p)


def _from_feature_major(cache_t):
    b, _, p = cache_t.shape
    return jnp.transpose(cache_t.reshape(b, N_HEADS_A, HEAD_DIM_A, p), (0, 3, 1, 2))


def kernel(x_prompt, x_sample, cache_k, cache_v, state_hgrn, w_in, w_out, rel_bias, lb_param, hgrn_norm_g, ln_g, ln_b):
    bsz, seq, _ = x_prompt.shape
    dec_b, dec_t, _ = x_sample.shape
    assert dec_t == DEC_SEQ and cache_k.shape[2] == WINDOW_MAX and w_in.shape[0] == DEPTH
    keep = min(WINDOW_MAX, seq)

    w_in16 = w_in[0].astype(BF16)
    w_out16 = w_out[0].astype(BF16)
    norm_g = hgrn_norm_g[0].reshape(1, HEAD_V_B).astype(F32)
    lng = ln_g[0].reshape(1, D_MODEL).astype(F32)
    lnb = ln_b[0].reshape(1, D_MODEL).astype(F32)
    lbp = lb_param.astype(F32)

    tm = PROJ_TILE
    n_p = bsz * seq
    main = lambda i: (i, 0)
    o_qa, o_ka, o_va, o_ga, o_qb, o_fb, o_ib, o_gb = (D_A * 0, D_A * 1, D_A * 2, D_A * 3, 4 * D_A,
                                                    4 * D_A + D_F, 4 * D_A + 2 * D_F, 4 * D_A + 2 * D_F + D_B)
    x2d = x_prompt.reshape(n_p, D_MODEL)
    proj = _proj_prompt(x2d, w_in16, bsz, seq, keep)
    n_cfg = len(DILATIONS)
    qs, ks, vs = proj[0:n_cfg], proj[n_cfg:2 * n_cfg], proj[2 * n_cfg:3 * n_cfg]
    kwin, vwin, ga, qb, fb, ib, gb = proj[3 * n_cfg:]

    attn_outs, lses = [], []
    for ci, dil in enumerate(DILATIONS):
        o, lse = _attn_prompt(qs[ci], ks[ci], vs[ci], _prompt_bias(rel_bias, dil), dil)
        attn_outs.append(o)
        lses.append(lse)

    n_s = dec_b * dec_t
    outs_s = [(lo, D_A, 1.0, F32, n_s, main) for lo in (o_qa, o_ka, o_va, o_ga, o_qb, o_fb, o_ib, o_gb)]
    xs2d = x_sample.reshape(n_s, D_MODEL)
    tm_s = min(tm, n_s)
    sqa, ska, sva, sga, sqb, sfb, sib, sgb = _proj(xs2d, w_in16, outs_s, tm_s)
    dist_s, masks_s = _sample_key_tables()
    base_s = jnp.transpose(_bias_by_distance(rel_bias, dist_s), (2, 0, 1))
    attn_s, k_new_t, v_new_t, ob, s_prompt = _attn_sample_hgrn_prompt(
        sqa, ska, sva, _to_feature_major(cache_k[0]), _to_feature_major(cache_v[0]), base_s, jnp.asarray(masks_s),
        qb, fb, ib, lbp, bsz, seq)
    y_prompt = _merge(x2d, attn_outs, lses, ga, ob, gb, w_out16, norm_g, lng, lnb, tm, seq).reshape(bsz, seq, D_MODEL)
    ob_s, s_sample = _hgrn_sample(sqb, sfb, sib, lbp, state_hgrn[0], LANES // DEC_SEQ)
    y_sample = _merge(xs2d, [attn_s], [], sga, ob_s, sgb, w_out16, norm_g, lng, lnb, tm_s).reshape(dec_b, dec_t, D_MODEL)

    return (y_prompt, y_sample,
            _from_feature_major(kwin)[None], _from_feature_major(vwin)[None],
            _from_feature_major(k_new_t)[None], _from_feature_major(v_new_t)[None],
            s_prompt[None], s_sample[None].astype(state_hgrn.dtype))
```

```python
import functools

import numpy as np
import jax
import jax.numpy as jnp
from jax import lax
from jax.experimental import pallas as pl
from jax.experimental.pallas import tpu as pltpu

F32 = jnp.float32
BF16 = jnp.bfloat16

D_MODEL = 1024
D_A = 512
HEAD_DIM_A = 64
N_HEADS_A = 8
DILATIONS = (1, 4, 16)
WIN_STEPS = 128
WINDOW_MAX = 2048
ATTN_BLOCK = 128
ATTN_SCALE = HEAD_DIM_A ** -0.5
NUM_BUCKETS = 32
REL_MAX_DIST = 2048
D_B = 512
N_HEADS_B = 4
HEAD_V_B = 128
EXPAND_B = 128
D_F = 512
GLA_CHUNK = 64
D_IN = 4 * D_A + 2 * D_F + 2 * D_B
DEPTH = 1
ALPHA = (2.0 * DEPTH) ** 0.25
NORM_EPS = 1e-5
DEC_SEQ = 8
LOG2_E = 1.4426950408889634
LN_2 = 0.6931471805599453

LANES = 128
MIB = 1024 * 1024

NT_DIMS = (((1,), (1,)), ((), ()))
TN_DIMS = (((0,), (0,)), ((), ()))


def _cparams(semantics, vmem_mib):
    return pltpu.CompilerParams(dimension_semantics=semantics, vmem_limit_bytes=vmem_mib * MIB)


def _rel_buckets(dist):
    max_exact = NUM_BUCKETS // 2
    d = np.maximum(dist, 1).astype(np.float32)
    large = max_exact + (np.log(d / max_exact) / np.log(REL_MAX_DIST / max_exact)
                         * (NUM_BUCKETS - max_exact)).astype(np.int32)
    large = np.minimum(large, NUM_BUCKETS - 1)
    return np.where(dist < max_exact, dist, large).astype(np.int32)


def _split3(x):
    hi = x.astype(BF16)
    r1 = x - hi.astype(F32)
    mid = r1.astype(BF16)
    lo = (r1 - mid.astype(F32)).astype(BF16)
    return hi, mid, lo


def _dot(a, b):
    return jnp.dot(a, b, preferred_element_type=F32)


def _dot3(mat_bf16, x):
    hi, mid, lo = _split3(x)
    return _dot(mat_bf16, hi) + _dot(mat_bf16, mid) + _dot(mat_bf16, lo)


def _split2(x):
    hi = x.astype(BF16)
    return hi, (x - hi.astype(F32)).astype(BF16)


def _dot2_rhs(x, mat_bf16):
    hi, lo = _split2(x)
    return _dot(hi, mat_bf16) + _dot(lo, mat_bf16)


def _dot2_lhs(mat_bf16, x):
    hi, lo = _split2(x)
    return _dot(mat_bf16, hi) + _dot(mat_bf16, lo)


def _sigmoid(x):
    return 0.5 * jnp.tanh(0.5 * x) + 0.5


def _silu(x):
    return x * _sigmoid(x)


def _bcast_rows(x, block, j):
    rows = x.shape[0]
    parts = [jnp.broadcast_to(x[r0 + j:r0 + j + 1, :], (block, x.shape[1])) for r0 in range(0, rows, block)]
    return parts[0] if len(parts) == 1 else jnp.concatenate(parts, axis=0)


def _proj_body(x_ref, w_ref, *out_refs, groups):
    x = x_ref[...].astype(BF16)
    for (lo, width, scale, idxs) in groups:
        h = _dot(x, w_ref[:, lo:lo + width])
        if scale != 1.0:
            h = h * scale
        for i, transposed in idxs:
            out_refs[i][...] = (h.T if transposed else h).astype(out_refs[i].dtype)


def _proj(x2d, w_bf16, outs, tm):
    n = x2d.shape[0]
    groups = {}
    for i, (lo, width, scale, _, rows, _) in enumerate(outs):
        groups.setdefault((lo, width, scale), []).append((i, isinstance(rows, tuple)))
    groups = tuple((lo, width, scale, tuple(idxs)) for (lo, width, scale), idxs in groups.items())
    out_shape, out_specs = [], []
    for (_, width, _, dt, rows, imap) in outs:
        if isinstance(rows, tuple):
            out_shape.append(jax.ShapeDtypeStruct((rows[0], width, rows[1]), dt))
            out_specs.append(pl.BlockSpec((None, width, tm), imap))
        else:
            out_shape.append(jax.ShapeDtypeStruct((rows, width), dt))
            out_specs.append(pl.BlockSpec((tm, width), imap))
    return pl.pallas_call(
        functools.partial(_proj_body, groups=groups),
        grid=(n // tm,),
        in_specs=[pl.BlockSpec((tm, D_MODEL), lambda i: (i, 0)),
                  pl.BlockSpec((D_MODEL, D_IN), lambda i: (0, 0))],
        out_specs=out_specs,
        out_shape=out_shape,
        compiler_params=_cparams(("arbitrary",), 56),
        name="proj",
    )(x2d, w_bf16)


PROJ_TILE = 512
N_PAIRS = D_A // LANES


def _blocked_shape(bsz, seq, dil, width, dtype):
    return jax.ShapeDtypeStruct((bsz, dil, seq // (dil * ATTN_BLOCK), ATTN_BLOCK, width), dtype)


def _blocked_tile_spec(seq, dil, width):
    tiles = seq // PROJ_TILE
    rows = PROJ_TILE // dil
    if rows >= ATTN_BLOCK:
        return pl.BlockSpec((None, dil, rows // ATTN_BLOCK, ATTN_BLOCK, width),
                            lambda i: (i // tiles, 0, i % tiles, 0, 0))
    per = ATTN_BLOCK // rows
    return pl.BlockSpec((None, dil, None, rows, width),
                        lambda i: (i // tiles, 0, (i % tiles) // per, (i % tiles) % per, 0))


def _store_blocked(out_refs, slabs, slabs4):
    assert DILATIONS == (1, 4, 16) and PROJ_TILE == 4 * ATTN_BLOCK
    out4, out16 = out_refs
    sub = ATTN_BLOCK // 4
    for j in range(slabs.shape[0]):
        lanes = slice(LANES * j, LANES * (j + 1))
        for c in range(4):
            piece = slabs[j, pl.ds(c, ATTN_BLOCK, stride=4), :]
            slabs4[j, c * ATTN_BLOCK:(c + 1) * ATTN_BLOCK, :] = piece
            out4[c, 0, :, lanes] = piece.astype(out4.dtype)
        for c in range(4):
            for r in range(4):
                piece = slabs4[j, pl.ds(c * ATTN_BLOCK + r, sub, stride=4), :]
                out16[c + 4 * r, :, lanes] = piece.astype(out16.dtype)


def _load_blocked(in_ref, slabs, dil, slabs4=None):
    sub = ATTN_BLOCK // 4
    for j in range(slabs.shape[0]):
        lanes = slice(LANES * j, LANES * (j + 1))
        for c in range(4):
            if dil == 4:
                piece = in_ref[c, 0, :, lanes].astype(F32)
            else:
                for r in range(4):
                    slabs4[j, pl.ds(c * ATTN_BLOCK + r, sub, stride=4), :] = in_ref[c + 4 * r, :, lanes].astype(F32)
                piece = slabs4[j, c * ATTN_BLOCK:(c + 1) * ATTN_BLOCK, :]
            slabs[j, pl.ds(c, ATTN_BLOCK, stride=4), :] = piece


def _proj_prompt_body(x_ref, w_ref, *refs):
    n_cfg = len(DILATIONS)
    blocked = [refs[n_cfg * g:n_cfg * (g + 1)] for g in range(3)]
    kwin_ref, vwin_ref, ga_ref, qb_ref, fb_ref, ib_ref, gb_ref = refs[3 * n_cfg:3 * n_cfg + 7]
    x16_ref = refs[3 * n_cfg + 7]
    slab_sets = refs[3 * n_cfg + 8:3 * n_cfg + 11]
    slab4_sets = refs[3 * n_cfg + 11:]
    x16_ref[...] = x_ref[...].astype(BF16)
    for g, scale in enumerate((ATTN_SCALE * LOG2_E, 1.0, 1.0)):
        h = _dot(x16_ref[...], w_ref[:, D_A * g:D_A * (g + 1)])
        if scale != 1.0:
            h = h * scale
        for j in range(N_PAIRS):
            slab_sets[g][j] = h[:, LANES * j:LANES * (j + 1)]
        for k in range(PROJ_TILE // ATTN_BLOCK):
            blocked[g][0][0, k] = h[k * ATTN_BLOCK:(k + 1) * ATTN_BLOCK, :].astype(BF16)
    lo = 3 * D_A
    for ref, width in ((ga_ref, D_A), (qb_ref, D_F), (fb_ref, D_F), (ib_ref, D_B), (gb_ref, D_B)):
        ref[...] = _dot(x16_ref[...], w_ref[:, lo:lo + width]).astype(ref.dtype)
        lo += width
    for g, win_ref in enumerate((None, kwin_ref, vwin_ref)):
        if win_ref is not None:
            for j in range(N_PAIRS):
                win_ref[LANES * j:LANES * (j + 1), :] = slab_sets[g][j].T
        _store_blocked(blocked[g][1:], slab_sets[g], slab4_sets[g])


def _proj_prompt(x2d, w_bf16, bsz, seq, keep):
    n = bsz * seq
    tiles = seq // PROJ_TILE
    win_tiles = keep // PROJ_TILE
    row = lambda width: pl.BlockSpec((PROJ_TILE, width), lambda i: (i, 0))
    win = pl.BlockSpec((None, D_A, PROJ_TILE),
                       lambda i: (i // tiles, 0, jnp.maximum(i % tiles - (tiles - win_tiles), 0)))
    out_shape, out_specs = [], []
    for _ in range(3):
        for dil in DILATIONS:
            out_shape.append(_blocked_shape(bsz, seq, dil, D_A, BF16))
            out_specs.append(_blocked_tile_spec(seq, dil, D_A))
    for _ in range(2):
        out_shape.append(jax.ShapeDtypeStruct((bsz, D_A, keep), F32))
        out_specs.append(win)
    for width, dt in ((D_A, BF16), (D_F, F32), (D_F, F32), (D_B, F32), (D_B, BF16)):
        out_shape.append(jax.ShapeDtypeStruct((n, width), dt))
        out_specs.append(row(width))
    return pl.pallas_call(
        _proj_prompt_body,
        grid=(n // PROJ_TILE,),
        in_specs=[row(D_MODEL), pl.BlockSpec((D_MODEL, D_IN), lambda i: (0, 0))],
        out_specs=out_specs,
        out_shape=out_shape,
        scratch_shapes=([pltpu.VMEM((PROJ_TILE, D_MODEL), BF16)]
                        + [pltpu.VMEM((N_PAIRS, PROJ_TILE, LANES), F32) for _ in range(6)]),
        compiler_params=_cparams(("arbitrary",), 56),
        name="proj_prompt",
    )(x2d, w_bf16)


UNITS_PER_STEP = 16


def _attn_unit(q, k_prev, k_cur, v_prev, v_cur, bias_ref, bias_set):
    lane = _iota2((ATTN_BLOCK, LANES), 1)
    low_half = lane < HEAD_DIM_A
    scores = []
    for j in range(N_PAIRS):
        lanes = slice(LANES * j, LANES * (j + 1))
        qj = q[:, lanes]
        zero = jnp.zeros_like(qj)
        kcat = jnp.concatenate([k_prev[:, lanes], k_cur[:, lanes]], axis=0)
        for qh in (jnp.where(low_half, qj, zero), jnp.where(low_half, zero, qj)):
            scores.append(lax.dot_general(qh, kcat, NT_DIMS, preferred_element_type=F32))

    probs, den_pairs = [], []
    m_tile = jnp.zeros((ATTN_BLOCK, LANES), F32)
    den_tile = jnp.ones((ATTN_BLOCK, LANES), F32)
    for h in range(N_HEADS_A):
        s = scores[h] + bias_ref[bias_set + h]
        m = jnp.max(jnp.maximum(s[:, :ATTN_BLOCK], s[:, ATTN_BLOCK:]), axis=1, keepdims=True)
        p = jnp.exp2(s - m)
        den = jnp.sum(p[:, :ATTN_BLOCK] + p[:, ATTN_BLOCK:], axis=1, keepdims=True)
        probs.append(p.astype(BF16))
        m_tile = jnp.where(lane == h, m, m_tile)
        den_tile = jnp.where(lane == h, den, den_tile)
        if h % 2 == 0:
            den_even = den
        else:
            den_pairs.append(jnp.where(low_half, den_even, den))
    lse_tile = (m_tile + jnp.log2(den_tile)) * LN_2

    outs = []
    for j in range(N_PAIRS):
        lanes = slice(LANES * j, LANES * (j + 1))
        vcat = jnp.concatenate([v_prev[:, lanes], v_cur[:, lanes]], axis=0)
        o_pair = jnp.where(low_half, _dot(probs[2 * j], vcat), _dot(probs[2 * j + 1], vcat))
        outs.append(o_pair / den_pairs[j])
    return outs, lse_tile


def _attn_prompt_body(q_ref, k_ref, v_ref, bias_ref, o_ref, lse_ref, kc_ref, vc_ref, *, n_res, n_blk):
    step = pl.program_id(2)

    @pl.when(step == 0)
    def _():
        kc_ref[...] = jnp.zeros_like(kc_ref)
        vc_ref[...] = jnp.zeros_like(vc_ref)

    def run(rr, n, k_prev, v_prev, bias_set):
        outs, lse_tile = _attn_unit(q_ref[rr, n], k_prev, k_ref[rr, n], v_prev, v_ref[rr, n], bias_ref, bias_set)
        for j in range(N_PAIRS):
            o_ref[rr, n, :, LANES * j:LANES * (j + 1)] = outs[j].astype(o_ref.dtype)
        lse_ref[rr, n] = lse_tile

    start_set = jnp.where(step == 0, N_HEADS_A, 0)
    for rr in range(n_res):
        run(rr, 0, kc_ref[rr], vc_ref[rr], start_set)

        def later(n, carry, rr=rr):
            run(rr, n, k_ref[rr, n - 1], v_ref[rr, n - 1], 0)
            return carry

        lax.fori_loop(1, n_blk, later, 0, unroll=5)
        kc_ref[rr] = k_ref[rr, n_blk - 1]
        vc_ref[rr] = v_ref[rr, n_blk - 1]


def _attn_prompt(q, k, v, bias, dil):
    b, _, nb, _, _ = q.shape
    n_blk = min(nb, UNITS_PER_STEP)
    n_res = min(dil, UNITS_PER_STEP // n_blk)
    blk = lambda width: pl.BlockSpec((None, n_res, n_blk, ATTN_BLOCK, width), lambda bi, ri, si: (bi, ri, si, 0, 0))
    return pl.pallas_call(
        functools.partial(_attn_prompt_body, n_res=n_res, n_blk=n_blk),
        grid=(b, dil // n_res, nb // n_blk),
        in_specs=[blk(D_A), blk(D_A), blk(D_A),
                  pl.BlockSpec((2 * N_HEADS_A, ATTN_BLOCK, 2 * ATTN_BLOCK), lambda bi, ri, si: (0, 0, 0))],
        out_specs=[blk(D_A), blk(LANES)],
        out_shape=[jax.ShapeDtypeStruct(q.shape, BF16),
                   jax.ShapeDtypeStruct(q.shape[:-1] + (LANES,), F32)],
        scratch_shapes=[pltpu.VMEM((n_res, ATTN_BLOCK, D_A), BF16), pltpu.VMEM((n_res, ATTN_BLOCK, D_A), BF16)],
        compiler_params=_cparams(("arbitrary", "arbitrary", "arbitrary"), 40),
        name=f"attn_prompt_d{dil}",
    )(q, k, v, bias)


def _lower_bound(lbp_ref):
    p0 = lbp_ref[0:1, :]
    p1 = lbp_ref[1:2, :]
    m = jnp.maximum(p0, p1)
    e0 = jnp.exp(p0 - m)
    e1 = jnp.exp(p1 - m)
    return e0 / (e0 + e1)


def _gates(qb, fb, lb):
    f = lb + (1.0 - lb) * _sigmoid(fb)
    return _silu(qb) * (EXPAND_B ** -0.5), 1.0 - f, jnp.log(f)


def _iota2(shape, dim):
    return lax.broadcasted_iota(jnp.int32, shape, dim)


def _div2(x, n):
    return jnp.right_shift(x, int(n).bit_length() - 1)


def _mod2(x, n):
    return jnp.bitwise_and(x, n - 1)


def _head_expand_matrix():
    return (_iota2((LANES, D_A), 0) == _div2(_iota2((LANES, D_A), 1), HEAD_DIM_A)).astype(BF16)


def _diag_block_scores(q, kk, b, a_heads, row, col, block):
    same = _div2(row, block) == _div2(col, block)
    for j in range(block):
        bj = _bcast_rows(b, block, j)
        kj = _bcast_rows(kk, block, j)
        x = q * jnp.exp(b - bj) * kj
        sel = same & (_mod2(col, block) == j) & (_mod2(row, block) >= j)
        for h in range(N_HEADS_B):
            cs = jnp.sum(x[:, h * EXPAND_B:(h + 1) * EXPAND_B], axis=-1, keepdims=True)
            a_heads[h] = jnp.where(sel, cs, a_heads[h])
    return a_heads


def _hgrn_prompt_tile(qb_ref, fb_ref, ib_ref, lbp_ref, o_ref, st_ref, q_s, kk_s, b_s, ref_s, qin_s, kout_s, v_s, a_s,
                      *, n_chunks):
    c_len = GLA_CHUNK
    chunks = [slice(c * c_len, (c + 1) * c_len) for c in range(n_chunks)]
    heads = [slice(h * EXPAND_B, (h + 1) * EXPAND_B) for h in range(N_HEADS_B)]

    lb = _lower_bound(lbp_ref)
    row = _iota2((c_len, c_len), 0)
    col = _iota2((c_len, c_len), 1)
    tri = (col <= row).astype(BF16)
    half_sizes = (32, 16, 8, 4, 2, 1)
    masks, picks = [], []
    for n in half_sizes:
        masks.append((_div2(row, 2 * n) == _div2(col, 2 * n)) & (_mod2(_div2(row, n), 2) == 1)
                     & (_mod2(_div2(col, n), 2) == 0))
        picks.append((col == _div2(row, 2 * n) * (2 * n) + (n - 1)).astype(BF16))
    pick_all = jnp.concatenate(picks, axis=0)
    on_diag = row == col

    for c, cs in enumerate(chunks):
        q, kk, g = _gates(qb_ref[cs, :], fb_ref[cs, :], lb)
        q_s[cs, :] = q
        kk_s[cs, :] = kk
        v_s[cs, :] = ib_ref[cs, :].astype(BF16)
        b_c = _dot2_lhs(tri, g)
        b_s[cs, :] = b_c
        refs = _dot(pick_all, b_c.astype(BF16))
        for li in range(len(half_sizes)):
            ref_s[li, cs, :] = refs[li * c_len:(li + 1) * c_len, :]
        b_end = b_c[c_len - 1:c_len, :]
        qin_s[cs, :] = (q * jnp.exp(b_c)).astype(BF16)
        kout_s[cs, :] = (kk * jnp.exp(b_end - b_c)).astype(BF16)
        qk = q * kk
        for h, hs in enumerate(heads):
            a_s[c, h] = jnp.where(on_diag, jnp.sum(qk[:, hs], axis=-1, keepdims=True), 0.0)
    for li, mask in enumerate(masks):
        for c, cs in enumerate(chunks):
            d = b_s[cs, :] - ref_s[li, cs, :]
            ql = (q_s[cs, :] * jnp.exp(d)).astype(BF16)
            kl = (kk_s[cs, :] * jnp.exp(-d)).astype(BF16)
            for h, hs in enumerate(heads):
                part = lax.dot_general(ql[:, hs], kl[:, hs], NT_DIMS, preferred_element_type=F32)
                a_s[c, h] = jnp.where(mask, part, a_s[c, h])

    for c, cs in enumerate(chunks):
        dec = jnp.exp(b_s[c * c_len + c_len - 1:(c + 1) * c_len, :])
        for h, hs in enumerate(heads):
            state = st_ref[h]
            inter = lax.dot_general(qin_s[cs, hs], state.astype(BF16), NT_DIMS, preferred_element_type=F32)
            intra = _dot(a_s[c, h].astype(BF16), v_s[cs, hs])
            o_ref[cs, hs] = (inter + intra).astype(o_ref.dtype)
            upd = lax.dot_general(v_s[cs, hs], kout_s[cs, hs], TN_DIMS, preferred_element_type=F32)
            st_ref[h] = state * dec[:, hs] + upd


def _hgrn_sample_body(qb_ref, fb_ref, ib_ref, lbp_ref, s0_ref, o_ref, s_ref, *, nbatch):
    rows_n = nbatch * DEC_SEQ
    assert rows_n == LANES
    lb = _lower_bound(lbp_ref)
    row = _iota2((rows_n, rows_n), 0)
    col = _iota2((rows_n, rows_n), 1)
    same = _div2(row, DEC_SEQ) == _div2(col, DEC_SEQ)
    tri = (same & (col <= row)).astype(BF16)
    pick = (col == row * DEC_SEQ + (DEC_SEQ - 1)).astype(BF16)

    q, kk, g = _gates(qb_ref[...], fb_ref[...], lb)
    v = ib_ref[...]
    b = _dot3(tri, g)
    b_end = _bcast_rows(b, DEC_SEQ, DEC_SEQ - 1)
    q_in = (q * jnp.exp(b)).astype(BF16)
    k_out = kk * jnp.exp(b_end - b)
    v16 = v.astype(BF16)

    a_heads = [jnp.zeros((rows_n, rows_n), F32) for _ in range(N_HEADS_B)]
    a_heads = _diag_block_scores(q, kk, b, a_heads, row, col, DEC_SEQ)

    rsel = _div2(_iota2((rows_n, 1), 0), DEC_SEQ)
    for h in range(N_HEADS_B):
        hs = slice(h * EXPAND_B, (h + 1) * EXPAND_B)
        intra = _dot(a_heads[h].astype(BF16), v16[:, hs])
        dec_t = jnp.exp(_dot3(pick, b[:, hs])).T
        k_out_t = k_out[:, hs].T.astype(BF16)
        for bi in range(nbatch):
            rs = slice(bi * DEC_SEQ, (bi + 1) * DEC_SEQ)
            s0 = s0_ref[bi, h]
            inter = _dot(q_in[rs, hs], s0.astype(BF16))
            o_ref[rs, hs] = inter + intra[rs, :]
            v_b = jnp.where(rsel == bi, v16[:, hs], jnp.zeros_like(v16[:, hs]))
            upd = _dot(k_out_t, v_b)
            s_ref[bi, h] = s0 * dec_t[:, bi:bi + 1] + upd


def _hgrn_sample(qb, fb, ib, lb_param, s0, nbatch):
    b = s0.shape[0]
    rows_n = nbatch * DEC_SEQ
    blk = pl.BlockSpec((rows_n, D_F), lambda i: (i, 0))
    sblk = pl.BlockSpec((nbatch, N_HEADS_B, EXPAND_B, HEAD_V_B), lambda i: (i, 0, 0, 0))
    return pl.pallas_call(
        functools.partial(_hgrn_sample_body, nbatch=nbatch),
        grid=(b // nbatch,),
        in_specs=[blk, blk, blk, pl.BlockSpec((DEPTH + 1, D_F), lambda i: (0, 0)), sblk],
        out_specs=[blk, sblk],
        out_shape=[jax.ShapeDtypeStruct((b * DEC_SEQ, D_B), F32),
                   jax.ShapeDtypeStruct(s0.shape, F32)],
        compiler_params=_cparams(("arbitrary",), 32),
        name="hgrn_sample",
    )(qb, fb, ib, lb_param, s0)


EXT_KEYS = WINDOW_MAX + LANES
SHIFT_ROWS = 64


def _sample_key_tables():
    t = np.arange(DEC_SEQ)[:, None]
    e = np.arange(EXT_KEYS)[None, :]
    is_new = e >= WINDOW_MAX
    d = np.where(is_new, t - (e - WINDOW_MAX), WINDOW_MAX + t - e)
    masks = np.full((len(DILATIONS), DEC_SEQ, EXT_KEYS), -np.inf, np.float32)
    for ci, dil in enumerate(DILATIONS):
        ok = (d >= 0) & (d % dil == 0) & (d <= WIN_STEPS * dil) & (e < WINDOW_MAX + DEC_SEQ)
        masks[ci][ok] = 0.0
    return np.maximum(d, 0), masks


def _attn_sample_body(q_ref, kn_ref, vn_ref, ck_ref, cv_ref, base_ref, mask_ref, attn_ref, ok_ref, ov_ref):
    keep = LANES - DEC_SEQ
    lane = _iota2((SHIFT_ROWS, LANES), 1)
    pad_rows = jnp.zeros((keep, D_A), F32)
    for c_ref, n_ref, o_ref in ((ck_ref, kn_ref, ok_ref), (cv_ref, vn_ref, ov_ref)):
        new_t = jnp.concatenate([pad_rows, n_ref[...]], axis=0).T
        for r0 in range(0, D_A, SHIFT_ROWS):
            rows = slice(r0, r0 + SHIFT_ROWS)
            rolled = pltpu.roll(c_ref[rows, :], WINDOW_MAX - DEC_SEQ, axis=1)
            o_ref[rows, 0:WINDOW_MAX - LANES] = rolled[:, 0:WINDOW_MAX - LANES]
            o_ref[rows, WINDOW_MAX - LANES:] = jnp.where(lane >= keep, new_t[rows, :], rolled[:, WINDOW_MAX - LANES:])

    qs = q_ref[...] * ATTN_SCALE
    pad_new = jnp.zeros((keep, HEAD_DIM_A), F32)
    for h in range(N_HEADS_A):
        hs = slice(HEAD_DIM_A * h, HEAD_DIM_A * (h + 1))
        qh = qs[:, hs].astype(BF16)
        s_cache = _dot(qh, ck_ref[hs, :].astype(BF16))
        kn_pad = jnp.concatenate([kn_ref[:, hs], pad_new], axis=0).astype(BF16)
        s_new = lax.dot_general(qh, kn_pad, NT_DIMS, preferred_element_type=F32)
        s = jnp.concatenate([s_cache, s_new], axis=1) + base_ref[h]
        probs, lses = [], []
        for ci in range(len(DILATIONS)):
            sc = s + mask_ref[ci]
            m = jnp.max(sc, axis=1, keepdims=True)
            p = jnp.exp(sc - m)
            den = jnp.sum(p, axis=1, keepdims=True)
            probs.append(p / den)
            lses.append(m + jnp.log(den))
        lmax = jnp.maximum(jnp.maximum(lses[0], lses[1]), lses[2])
        ws = [jnp.exp(l - lmax) for l in lses]
        wsum = ws[0] + ws[1] + ws[2]
        pw = (probs[0] * (ws[0] / wsum) + probs[1] * (ws[1] / wsum) + probs[2] * (ws[2] / wsum)).astype(BF16)
        o = lax.dot_general(pw[:, 0:WINDOW_MAX], cv_ref[hs, :].astype(BF16), NT_DIMS, preferred_element_type=F32)
        vn_pad = jnp.concatenate([vn_ref[:, hs], pad_new], axis=0).astype(BF16)
        attn_ref[:, hs] = o + _dot(pw[:, WINDOW_MAX:], vn_pad)


N_ATTN_SAMPLE_IN = 7
N_ATTN_SAMPLE_OUT = 3


def _attn_sample_hgrn_prompt_body(*refs, n_chunks, tiles_per_seq):
    n_in = N_ATTN_SAMPLE_IN + 4
    n_out = N_ATTN_SAMPLE_OUT + 2
    ins, outs, scratch = refs[:n_in], refs[n_in:n_in + n_out], refs[n_in + n_out:]
    o_ref, s_ref = outs[N_ATTN_SAMPLE_OUT:]
    st_ref = scratch[0]
    t = lax.rem(pl.program_id(0), tiles_per_seq)

    @pl.when(t == 0)
    def _():
        st_ref[...] = jnp.zeros_like(st_ref)

    _attn_sample_body(*ins[:N_ATTN_SAMPLE_IN], *outs[:N_ATTN_SAMPLE_OUT])
    _hgrn_prompt_tile(*ins[N_ATTN_SAMPLE_IN:], o_ref, *scratch, n_chunks=n_chunks)

    @pl.when(t == tiles_per_seq - 1)
    def _():
        for h in range(N_HEADS_B):
            s_ref[h] = st_ref[h].T


def _attn_sample_hgrn_prompt(q, kn, vn, cache_kt, cache_vt, base, masks, qb, fb, ib, lb_param, bsz, seq):
    dec_b = cache_kt.shape[0]
    tt = bsz * seq // dec_b
    assert bsz * seq == tt * dec_b and seq % tt == 0 and tt % GLA_CHUNK == 0
    n_chunks = tt // GLA_CHUNK
    small = pl.BlockSpec((DEC_SEQ, D_A), lambda i: (i, 0))
    big = pl.BlockSpec((None, D_A, WINDOW_MAX), lambda i: (i, 0, 0))
    const = lambda a: pl.BlockSpec(a.shape, lambda i: (0,) * a.ndim)
    blk = pl.BlockSpec((tt, D_F), lambda i: (i, 0))
    state_blk = pl.BlockSpec((None, N_HEADS_B, EXPAND_B, HEAD_V_B), lambda i: (i // (seq // tt), 0, 0, 0))
    tile = lambda dt: pltpu.VMEM((tt, D_F), dt)
    return pl.pallas_call(
        functools.partial(_attn_sample_hgrn_prompt_body, n_chunks=n_chunks, tiles_per_seq=seq // tt),
        grid=(dec_b,),
        in_specs=[small, small, small, big, big, const(base), const(masks), blk, blk, blk, const(lb_param)],
        out_specs=[small, big, big, blk, state_blk],
        out_shape=[jax.ShapeDtypeStruct((dec_b * DEC_SEQ, D_A), F32),
                   jax.ShapeDtypeStruct(cache_kt.shape, F32),
                   jax.ShapeDtypeStruct(cache_vt.shape, F32),
                   jax.ShapeDtypeStruct((bsz * seq, D_B), BF16),
                   jax.ShapeDtypeStruct((bsz, N_HEADS_B, EXPAND_B, HEAD_V_B), F32)],
        scratch_shapes=[pltpu.VMEM((N_HEADS_B, HEAD_V_B, EXPAND_B), F32),
                        tile(F32), tile(F32), tile(F32),
                        pltpu.VMEM((6, tt, D_F), F32),
                        tile(BF16), tile(BF16), tile(BF16),
                        pltpu.VMEM((n_chunks, N_HEADS_B, GLA_CHUNK, GLA_CHUNK), F32)],
        compiler_params=_cparams(("arbitrary",), 56),
        name="attn_sample_hgrn_prompt",
    )(q, kn, vn, cache_kt, cache_vt, base, masks, qb, fb, ib, lb_param)


def _merge_body(*refs, n_cfg):
    x_ref = refs[0]
    pairs = [slice(LANES * j, LANES * (j + 1)) for j in range(N_PAIRS)]
    if n_cfg > 1:
        o_refs = refs[1:1 + n_cfg]
        l_refs = refs[1 + n_cfg:1 + 2 * n_cfg]
        tail = refs[1 + 2 * n_cfg:9 + 2 * n_cfg]
        scratch = refs[9 + 2 * n_cfg:]
        o_nat, l_nat = [], []
        for ci, dil in enumerate(DILATIONS):
            if dil == 1:
                blocks = range(PROJ_TILE // ATTN_BLOCK)
                o_val = jnp.concatenate([o_refs[ci][0, k] for k in blocks], axis=0).astype(F32)
                o_nat.append([o_val[:, lanes] for lanes in pairs])
                l_nat.append(jnp.concatenate([l_refs[ci][0, k] for k in blocks], axis=0))
            else:
                o_slabs, l_slab = scratch[2 * (ci - 1)], scratch[2 * (ci - 1) + 1]
                o_tmp, l_tmp = scratch[-2:]
                _load_blocked(o_refs[ci], o_slabs, dil, o_tmp)
                _load_blocked(l_refs[ci], l_slab, dil, l_tmp)
                o_nat.append([o_slabs[j] for j in range(N_PAIRS)])
                l_nat.append(l_slab[0])
        et_mat = _head_expand_matrix()
        lmax = functools.reduce(jnp.maximum, l_nat)
        ws = [jnp.exp(l - lmax) for l in l_nat]
        wsum = functools.reduce(lambda a, c: a + c, ws)
        attn = [None] * N_PAIRS
        for w, o_c in zip(ws, o_nat):
            w_lanes = _dot2_rhs(w / wsum, et_mat)
            for j, lanes in enumerate(pairs):
                term = w_lanes[:, lanes] * o_c[j]
                attn[j] = term if attn[j] is None else attn[j] + term
    else:
        tail = refs[2:10]
        attn = [refs[1][:, lanes].astype(F32) for lanes in pairs]
    ga_ref, ob_ref, gb_ref, wout_ref, ng_ref, lg_ref, lbias_ref, y_ref = tail

    parts = [(attn[j] * _silu(ga_ref[:, lanes].astype(F32))).astype(BF16) for j, lanes in enumerate(pairs)]
    for h in range(N_HEADS_B):
        hs = slice(h * HEAD_V_B, (h + 1) * HEAD_V_B)
        oh = ob_ref[:, hs].astype(F32)
        ms = jnp.mean(oh * oh, axis=-1, keepdims=True)
        on = oh * lax.rsqrt(ms + NORM_EPS) * ng_ref[...]
        parts.append((on * _silu(gb_ref[:, hs].astype(F32))).astype(BF16))
    mix = jnp.concatenate(parts, axis=-1)
    z = ALPHA * x_ref[...] + _dot(mix, wout_ref[...])
    mu = jnp.mean(z, axis=-1, keepdims=True)
    zc = z - mu
    var = jnp.mean(zc * zc, axis=-1, keepdims=True)
    y_ref[...] = zc * lax.rsqrt(var + NORM_EPS) * lg_ref[...] + lbias_ref[...]


def _merge(x2d, attn_outs, lses, ga, ob, gb, w_out_bf16, norm_g, ln_g, ln_b, tm, seq=None):
    n = x2d.shape[0]
    n_cfg = len(attn_outs)
    row = lambda width: pl.BlockSpec((tm, width), lambda i: (i, 0))
    const = lambda shape: pl.BlockSpec(shape, lambda i: (0,) * len(shape))
    if n_cfg > 1:
        assert tm == PROJ_TILE and n_cfg == len(DILATIONS)
        attn_specs = ([_blocked_tile_spec(seq, dil, D_A) for dil in DILATIONS]
                      + [_blocked_tile_spec(seq, dil, LANES) for dil in DILATIONS])
        scratch = []
        for dil in DILATIONS[1:] + (0,):
            scratch += [pltpu.VMEM((N_PAIRS, tm, LANES), F32), pltpu.VMEM((1, tm, LANES), F32)]
    else:
        attn_specs, scratch = [row(D_A)], []
    in_specs = ([row(D_MODEL)] + attn_specs
                + [row(D_A), row(D_B), row(D_B), const((D_MODEL, D_MODEL)),
                   const((1, HEAD_V_B)), const((1, D_MODEL)), const((1, D_MODEL))])
    return pl.pallas_call(
        functools.partial(_merge_body, n_cfg=n_cfg),
        grid=(n // tm,),
        in_specs=in_specs,
        out_specs=row(D_MODEL),
        out_shape=jax.ShapeDtypeStruct((n, D_MODEL), F32),
        scratch_shapes=scratch,
        compiler_params=_cparams(("arbitrary",), 48),
        name=f"merge_{n_cfg}",
    )(x2d, *attn_outs, *lses, ga, ob, gb, w_out_bf16, norm_g, ln_g, ln_b)


def _bias_by_distance(rel_bias, dist):
    onehot = jax.nn.one_hot(_rel_buckets(dist.reshape(-1)), NUM_BUCKETS, dtype=F32)
    table = jnp.dot(onehot, rel_bias.astype(F32), precision=lax.Precision.HIGHEST)
    return table.reshape(dist.shape + (N_HEADS_A,))


def _prompt_bias(rel_bias, dil):
    i = np.arange(ATTN_BLOCK)[:, None]
    j = np.arange(2 * ATTN_BLOCK)[None, :]
    sub = ATTN_BLOCK + i - j
    ok = (sub >= 0) & (sub <= WIN_STEPS)
    bias = _bias_by_distance(rel_bias, dil * np.clip(sub, 0, WIN_STEPS)) * LOG2_E
    full = jnp.where(ok[:, :, None], bias, -jnp.inf)
    no_prev = jnp.where((ok & (j >= ATTN_BLOCK))[:, :, None], bias, -jnp.inf)
    return jnp.transpose(jnp.concatenate([full, no_prev], axis=2), (2, 0, 1))


def _to_feature_major(cache):
    b, p, h, d = cache.shape
    return jnp.transpose(cache, (0, 2, 3, 1)).reshape(b, h * d, p)


def _from_feature_major(cache_t):
    b, _, p = cache_t.shape
    return jnp.transpose(cache_t.reshape(b, N_HEADS_A, HEAD_DIM_A, p), (0, 3, 1, 2))


def kernel(x_prompt, x_sample, cache_k, cache_v, state_hgrn, w_in, w_out, rel_bias, lb_param, hgrn_norm_g, ln_g, ln_b):
    bsz, seq, _ = x_prompt.shape
    dec_b, dec_t, _ = x_sample.shape
    assert dec_t == DEC_SEQ and cache_k.shape[2] == WINDOW_MAX and w_in.shape[0] == DEPTH
    keep = min(WINDOW_MAX, seq)

    w_in16 = w_in[0].astype(BF16)
    w_out16 = w_out[0].astype(BF16)
    norm_g = hgrn_norm_g[0].reshape(1, HEAD_V_B).astype(F32)
    lng = ln_g[0].reshape(1, D_MODEL).astype(F32)
    lnb = ln_b[0].reshape(1, D_MODEL).astype(F32)
    lbp = lb_param.astype(F32)

    tm = PROJ_TILE
    n_p = bsz * seq
    main = lambda i: (i, 0)
    o_qa, o_ka, o_va, o_ga, o_qb, o_fb, o_ib, o_gb = (D_A * 0, D_A * 1, D_A * 2, D_A * 3, 4 * D_A,
                                                    4 * D_A + D_F, 4 * D_A + 2 * D_F, 4 * D_A + 2 * D_F + D_B)
    x2d = x_prompt.reshape(n_p, D_MODEL)
    proj = _proj_prompt(x2d, w_in16, bsz, seq, keep)
    n_cfg = len(DILATIONS)
    qs, ks, vs = proj[0:n_cfg], proj[n_cfg:2 * n_cfg], proj[2 * n_cfg:3 * n_cfg]
    kwin, vwin, ga, qb, fb, ib, gb = proj[3 * n_cfg:]

    attn_outs, lses = [], []
    for ci, dil in enumerate(DILATIONS):
        o, lse = _attn_prompt(qs[ci], ks[ci], vs[ci], _prompt_bias(rel_bias, dil), dil)
        attn_outs.append(o)
        lses.append(lse)

    n_s = dec_b * dec_t
    outs_s = [(lo, D_A, 1.0, F32, n_s, main) for lo in (o_qa, o_ka, o_va, o_ga, o_qb, o_fb, o_ib, o_gb)]
    xs2d = x_sample.reshape(n_s, D_MODEL)
    tm_s = min(tm, n_s)
    sqa, ska, sva, sga, sqb, sfb, sib, sgb = _proj(xs2d, w_in16, outs_s, tm_s)
    dist_s, masks_s = _sample_key_tables()
    base_s = jnp.transpose(_bias_by_distance(rel_bias, dist_s), (2, 0, 1))
    attn_s, k_new_t, v_new_t, ob, s_prompt = _attn_sample_hgrn_prompt(
        sqa, ska, sva, _to_feature_major(cache_k[0]), _to_feature_major(cache_v[0]), base_s, jnp.asarray(masks_s),
        qb, fb, ib, lbp, bsz, seq)
    y_prompt = _merge(x2d, attn_outs, lses, ga, ob, gb, w_out16, norm_g, lng, lnb, tm, seq).reshape(bsz, seq, D_MODEL)
    ob_s, s_sample = _hgrn_sample(sqb, sfb, sib, lbp, state_hgrn[0], LANES // DEC_SEQ)
    y_sample = _merge(xs2d, [attn_s], [], sga, ob_s, sgb, w_out16, norm_g, lng, lnb, tm_s).reshape(dec_b, dec_t, D_MODEL)

    return (y_prompt, y_sample,
            _from_feature_major(kwin)[None], _from_feature_major(vwin)[None],
            _from_feature_major(k_new_t)[None], _from_feature_major(v_new_t)[None],
            s_prompt[None], s_sample[None].astype(state_hgrn.dtype))
```

```python
import functools

import numpy as np
import jax
import jax.numpy as jnp
from jax import lax
from jax.experimental import pallas as pl
from jax.experimental.pallas import tpu as pltpu

F32 = jnp.float32
BF16 = jnp.bfloat16

D_MODEL = 1024
D_A = 512
HEAD_DIM_A = 64
N_HEADS_A = 8
DILATIONS = (1, 4, 16)
WIN_STEPS = 128
WINDOW_MAX = 2048
ATTN_BLOCK = 128
ATTN_SCALE = HEAD_DIM_A ** -0.5
NUM_BUCKETS = 32
REL_MAX_DIST = 2048
D_B = 512
N_HEADS_B = 4
HEAD_V_B = 128
EXPAND_B = 128
D_F = 512
GLA_CHUNK = 64
D_IN = 4 * D_A + 2 * D_F + 2 * D_B
DEPTH = 1
ALPHA = (2.0 * DEPTH) ** 0.25
NORM_EPS = 1e-5
DEC_SEQ = 8
LOG2_E = 1.4426950408889634
LN_2 = 0.6931471805599453

LANES = 128
MIB = 1024 * 1024

NT_DIMS = (((1,), (1,)), ((), ()))
TN_DIMS = (((0,), (0,)), ((), ()))


def _cparams(semantics, vmem_mib):
    return pltpu.CompilerParams(dimension_semantics=semantics, vmem_limit_bytes=vmem_mib * MIB)


def _rel_buckets(dist):
    max_exact = NUM_BUCKETS // 2
    d = np.maximum(dist, 1).astype(np.float32)
    large = max_exact + (np.log(d / max_exact) / np.log(REL_MAX_DIST / max_exact)
                         * (NUM_BUCKETS - max_exact)).astype(np.int32)
    large = np.minimum(large, NUM_BUCKETS - 1)
    return np.where(dist < max_exact, dist, large).astype(np.int32)


def _split3(x):
    hi = x.astype(BF16)
    r1 = x - hi.astype(F32)
    mid = r1.astype(BF16)
    lo = (r1 - mid.astype(F32)).astype(BF16)
    return hi, mid, lo


def _dot(a, b):
    return jnp.dot(a, b, preferred_element_type=F32)


def _dot3(mat_bf16, x):
    hi, mid, lo = _split3(x)
    return _dot(mat_bf16, hi) + _dot(mat_bf16, mid) + _dot(mat_bf16, lo)


def _split2(x):
    hi = x.astype(BF16)
    return hi, (x - hi.astype(F32)).astype(BF16)


def _dot2_rhs(x, mat_bf16):
    hi, lo = _split2(x)
    return _dot(hi, mat_bf16) + _dot(lo, mat_bf16)


def _dot2_lhs(mat_bf16, x):
    hi, lo = _split2(x)
    return _dot(mat_bf16, hi) + _dot(mat_bf16, lo)


def _sigmoid(x):
    return 0.5 * jnp.tanh(0.5 * x) + 0.5


def _silu(x):
    return x * _sigmoid(x)


def _bcast_rows(x, block, j):
    rows = x.shape[0]
    parts = [jnp.broadcast_to(x[r0 + j:r0 + j + 1, :], (block, x.shape[1])) for r0 in range(0, rows, block)]
    return parts[0] if len(parts) == 1 else jnp.concatenate(parts, axis=0)


def _proj_body(x_ref, w_ref, *out_refs, groups):
    x = x_ref[...].astype(BF16)
    for (lo, width, scale, idxs) in groups:
        h = _dot(x, w_ref[:, lo:lo + width])
        if scale != 1.0:
            h = h * scale
        for i, transposed in idxs:
            out_refs[i][...] = (h.T if transposed else h).astype(out_refs[i].dtype)


def _proj(x2d, w_bf16, outs, tm):
    n = x2d.shape[0]
    groups = {}
    for i, (lo, width, scale, _, rows, _) in enumerate(outs):
        groups.setdefault((lo, width, scale), []).append((i, isinstance(rows, tuple)))
    groups = tuple((lo, width, scale, tuple(idxs)) for (lo, width, scale), idxs in groups.items())
    out_shape, out_specs = [], []
    for (_, width, _, dt, rows, imap) in outs:
        if isinstance(rows, tuple):
            out_shape.append(jax.ShapeDtypeStruct((rows[0], width, rows[1]), dt))
            out_specs.append(pl.BlockSpec((None, width, tm), imap))
        else:
            out_shape.append(jax.ShapeDtypeStruct((rows, width), dt))
            out_specs.append(pl.BlockSpec((tm, width), imap))
    return pl.pallas_call(
        functools.partial(_proj_body, groups=groups),
        grid=(n // tm,),
        in_specs=[pl.BlockSpec((tm, D_MODEL), lambda i: (i, 0)),
                  pl.BlockSpec((D_MODEL, D_IN), lambda i: (0, 0))],
        out_specs=out_specs,
        out_shape=out_shape,
        compiler_params=_cparams(("arbitrary",), 56),
        name="proj",
    )(x2d, w_bf16)


PROJ_TILE = 512
N_PAIRS = D_A // LANES


def _blocked_shape(bsz, seq, dil, width, dtype):
    return jax.ShapeDtypeStruct((bsz, dil, seq // (dil * ATTN_BLOCK), ATTN_BLOCK, width), dtype)


def _blocked_tile_spec(seq, dil, width):
    tiles = seq // PROJ_TILE
    rows = PROJ_TILE // dil
    if rows >= ATTN_BLOCK:
        return pl.BlockSpec((None, dil, rows // ATTN_BLOCK, ATTN_BLOCK, width),
                            lambda i: (i // tiles, 0, i % tiles, 0, 0))
    per = ATTN_BLOCK // rows
    return pl.BlockSpec((None, dil, None, rows, width),
                        lambda i: (i // tiles, 0, (i % tiles) // per, (i % tiles) % per, 0))


def _store_blocked(out_refs, slabs, slabs4):
    assert DILATIONS == (1, 4, 16) and PROJ_TILE == 4 * ATTN_BLOCK
    out4, out16 = out_refs
    sub = ATTN_BLOCK // 4
    for j in range(slabs.shape[0]):
        lanes = slice(LANES * j, LANES * (j + 1))
        for c in range(4):
            piece = slabs[j, pl.ds(c, ATTN_BLOCK, stride=4), :]
            slabs4[j, c * ATTN_BLOCK:(c + 1) * ATTN_BLOCK, :] = piece
            out4[c, 0, :, lanes] = piece.astype(out4.dtype)
        for c in range(4):
            for r in range(4):
                piece = slabs4[j, pl.ds(c * ATTN_BLOCK + r, sub, stride=4), :]
                out16[c + 4 * r, :, lanes] = piece.astype(out16.dtype)


def _load_blocked(in_ref, slabs, dil, slabs4=None):
    sub = ATTN_BLOCK // 4
    for j in range(slabs.shape[0]):
        lanes = slice(LANES * j, LANES * (j + 1))
        for c in range(4):
            if dil == 4:
                piece = in_ref[c, 0, :, lanes].astype(F32)
            else:
                for r in range(4):
                    slabs4[j, pl.ds(c * ATTN_BLOCK + r, sub, stride=4), :] = in_ref[c + 4 * r, :, lanes].astype(F32)
                piece = slabs4[j, c * ATTN_BLOCK:(c + 1) * ATTN_BLOCK, :]
            slabs[j, pl.ds(c, ATTN_BLOCK, stride=4), :] = piece


def _proj_prompt_body(x_ref, w_ref, *refs):
    n_cfg = len(DILATIONS)
    blocked = [refs[n_cfg * g:n_cfg * (g + 1)] for g in range(3)]
    kwin_ref, vwin_ref, ga_ref, qb_ref, fb_ref, ib_ref, gb_ref = refs[3 * n_cfg:3 * n_cfg + 7]
    x16_ref = refs[3 * n_cfg + 7]
    slab_sets = refs[3 * n_cfg + 8:3 * n_cfg + 11]
    slab4_sets = refs[3 * n_cfg + 11:]
    x16_ref[...] = x_ref[...].astype(BF16)
    for g, scale in enumerate((ATTN_SCALE * LOG2_E, 1.0, 1.0)):
        h = _dot(x16_ref[...], w_ref[:, D_A * g:D_A * (g + 1)])
        if scale != 1.0:
            h = h * scale
        for j in range(N_PAIRS):
            slab_sets[g][j] = h[:, LANES * j:LANES * (j + 1)]
        for k in range(PROJ_TILE // ATTN_BLOCK):
            blocked[g][0][0, k] = h[k * ATTN_BLOCK:(k + 1) * ATTN_BLOCK, :].astype(BF16)
    lo = 3 * D_A
    for ref, width in ((ga_ref, D_A), (qb_ref, D_F), (fb_ref, D_F), (ib_ref, D_B), (gb_ref, D_B)):
        ref[...] = _dot(x16_ref[...], w_ref[:, lo:lo + width]).astype(ref.dtype)
        lo += width
    for g, win_ref in enumerate((None, kwin_ref, vwin_ref)):
        if win_ref is not None:
            for j in range(N_PAIRS):
                win_ref[LANES * j:LANES * (j + 1), :] = slab_sets[g][j].T
        _store_blocked(blocked[g][1:], slab_sets[g], slab4_sets[g])


def _proj_prompt(x2d, w_bf16, bsz, seq, keep):
    n = bsz * seq
    tiles = seq // PROJ_TILE
    win_tiles = keep // PROJ_TILE
    row = lambda width: pl.BlockSpec((PROJ_TILE, width), lambda i: (i, 0))
    win = pl.BlockSpec((None, D_A, PROJ_TILE),
                       lambda i: (i // tiles, 0, jnp.maximum(i % tiles - (tiles - win_tiles), 0)))
    out_shape, out_specs = [], []
    for _ in range(3):
        for dil in DILATIONS:
            out_shape.append(_blocked_shape(bsz, seq, dil, D_A, BF16))
            out_specs.append(_blocked_tile_spec(seq, dil, D_A))
    for _ in range(2):
        out_shape.append(jax.ShapeDtypeStruct((bsz, D_A, keep), F32))
        out_specs.append(win)
    for width, dt in ((D_A, BF16), (D_F, F32), (D_F, F32), (D_B, F32), (D_B, BF16)):
        out_shape.append(jax.ShapeDtypeStruct((n, width), dt))
        out_specs.append(row(width))
    return pl.pallas_call(
        _proj_prompt_body,
        grid=(n // PROJ_TILE,),
        in_specs=[row(D_MODEL), pl.BlockSpec((D_MODEL, D_IN), lambda i: (0, 0))],
        out_specs=out_specs,
        out_shape=out_shape,
        scratch_shapes=([pltpu.VMEM((PROJ_TILE, D_MODEL), BF16)]
                        + [pltpu.VMEM((N_PAIRS, PROJ_TILE, LANES), F32) for _ in range(6)]),
        compiler_params=_cparams(("arbitrary",), 56),
        name="proj_prompt",
    )(x2d, w_bf16)


UNITS_PER_STEP = 16


def _attn_unit(q, k_prev, k_cur, v_prev, v_cur, bias_ref, bias_set):
    lane = _iota2((ATTN_BLOCK, LANES), 1)
    low_half = lane < HEAD_DIM_A
    scores = []
    for j in range(N_PAIRS):
        lanes = slice(LANES * j, LANES * (j + 1))
        qj = q[:, lanes]
        zero = jnp.zeros_like(qj)
        kcat = jnp.concatenate([k_prev[:, lanes], k_cur[:, lanes]], axis=0)
        for qh in (jnp.where(low_half, qj, zero), jnp.where(low_half, zero, qj)):
            scores.append(lax.dot_general(qh, kcat, NT_DIMS, preferred_element_type=F32))

    probs, den_pairs = [], []
    m_tile = jnp.zeros((ATTN_BLOCK, LANES), F32)
    den_tile = jnp.ones((ATTN_BLOCK, LANES), F32)
    for h in range(N_HEADS_A):
        s = scores[h] + bias_ref[bias_set + h]
        m = jnp.max(jnp.maximum(s[:, :ATTN_BLOCK], s[:, ATTN_BLOCK:]), axis=1, keepdims=True)
        p = jnp.exp2(s - m)
        den = jnp.sum(p[:, :ATTN_BLOCK] + p[:, ATTN_BLOCK:], axis=1, keepdims=True)
        probs.append(p.astype(BF16))
        m_tile = jnp.where(lane == h, m, m_tile)
        den_tile = jnp.where(lane == h, den, den_tile)
        if h % 2 == 0:
            den_even = den
        else:
            den_pairs.append(jnp.where(low_half, den_even, den))
    lse_tile = (m_tile + jnp.log2(den_tile)) * LN_2

    outs = []
    for j in range(N_PAIRS):
        lanes = slice(LANES * j, LANES * (j + 1))
        vcat = jnp.concatenate([v_prev[:, lanes], v_cur[:, lanes]], axis=0)
        o_pair = jnp.where(low_half, _dot(probs[2 * j], vcat), _dot(probs[2 * j + 1], vcat))
        outs.append(o_pair / den_pairs[j])
    return outs, lse_tile


def _attn_prompt_body(q_ref, k_ref, v_ref, bias_ref, o_ref, lse_ref, kc_ref, vc_ref, *, n_res, n_blk):
    step = pl.program_id(2)
    slot = lax.rem(step, 2)

    @pl.when(step == 0)
    def _():
        kc_ref[0] = jnp.zeros_like(kc_ref[0])
        vc_ref[0] = jnp.zeros_like(vc_ref[0])

    def run(rr, n, k_prev, v_prev, bias_set):
        outs, lse_tile = _attn_unit(q_ref[rr, n], k_prev, k_ref[rr, n], v_prev, v_ref[rr, n], bias_ref, bias_set)
        for j in range(N_PAIRS):
            o_ref[rr, n, :, LANES * j:LANES * (j + 1)] = outs[j].astype(o_ref.dtype)
        lse_ref[rr, n] = lse_tile

    start_set = jnp.where(step == 0, N_HEADS_A, 0)
    for rr in range(n_res):
        run(rr, 0, kc_ref[slot, rr], vc_ref[slot, rr], start_set)

        def later(n, carry, rr=rr):
            run(rr, n, k_ref[rr, n - 1], v_ref[rr, n - 1], 0)
            return carry

        lax.fori_loop(1, n_blk, later, 0, unroll=True)
        kc_ref[1 - slot, rr] = k_ref[rr, n_blk - 1]
        vc_ref[1 - slot, rr] = v_ref[rr, n_blk - 1]


def _attn_prompt(q, k, v, bias, dil):
    b, _, nb, _, _ = q.shape
    n_blk = min(nb, UNITS_PER_STEP)
    n_res = min(dil, UNITS_PER_STEP // n_blk)
    blk = lambda width: pl.BlockSpec((None, n_res, n_blk, ATTN_BLOCK, width), lambda bi, ri, si: (bi, ri, si, 0, 0))
    return pl.pallas_call(
        functools.partial(_attn_prompt_body, n_res=n_res, n_blk=n_blk),
        grid=(b, dil // n_res, nb // n_blk),
        in_specs=[blk(D_A), blk(D_A), blk(D_A),
                  pl.BlockSpec((2 * N_HEADS_A, ATTN_BLOCK, 2 * ATTN_BLOCK), lambda bi, ri, si: (0, 0, 0))],
        out_specs=[blk(D_A), blk(LANES)],
        out_shape=[jax.ShapeDtypeStruct(q.shape, BF16),
                   jax.ShapeDtypeStruct(q.shape[:-1] + (LANES,), F32)],
        scratch_shapes=[pltpu.VMEM((2, n_res, ATTN_BLOCK, D_A), BF16), pltpu.VMEM((2, n_res, ATTN_BLOCK, D_A), BF16)],
        compiler_params=_cparams(("arbitrary", "arbitrary", "arbitrary"), 40),
        name=f"attn_prompt_d{dil}",
    )(q, k, v, bias)


def _lower_bound(lbp_ref):
    p0 = lbp_ref[0:1, :]
    p1 = lbp_ref[1:2, :]
    m = jnp.maximum(p0, p1)
    e0 = jnp.exp(p0 - m)
    e1 = jnp.exp(p1 - m)
    return e0 / (e0 + e1)


def _gates(qb, fb, lb):
    f = lb + (1.0 - lb) * _sigmoid(fb)
    return _silu(qb) * (EXPAND_B ** -0.5), 1.0 - f, jnp.log(f)


def _iota2(shape, dim):
    return lax.broadcasted_iota(jnp.int32, shape, dim)


def _div2(x, n):
    return jnp.right_shift(x, int(n).bit_length() - 1)


def _mod2(x, n):
    return jnp.bitwise_and(x, n - 1)


def _head_expand_matrix():
    return (_iota2((LANES, D_A), 0) == _div2(_iota2((LANES, D_A), 1), HEAD_DIM_A)).astype(BF16)


def _diag_block_scores(q, kk, b, a_heads, row, col, block):
    same = _div2(row, block) == _div2(col, block)
    for j in range(block):
        bj = _bcast_rows(b, block, j)
        kj = _bcast_rows(kk, block, j)
        x = q * jnp.exp(b - bj) * kj
        sel = same & (_mod2(col, block) == j) & (_mod2(row, block) >= j)
        for h in range(N_HEADS_B):
            cs = jnp.sum(x[:, h * EXPAND_B:(h + 1) * EXPAND_B], axis=-1, keepdims=True)
            a_heads[h] = jnp.where(sel, cs, a_heads[h])
    return a_heads


def _hgrn_prompt_tile(qb_ref, fb_ref, ib_ref, lbp_ref, o_ref, st_ref, q_s, kk_s, b_s, ref_s, qin_s, kout_s, v_s, a_s,
                      *, n_chunks):
    c_len = GLA_CHUNK
    chunks = [slice(c * c_len, (c + 1) * c_len) for c in range(n_chunks)]
    heads = [slice(h * EXPAND_B, (h + 1) * EXPAND_B) for h in range(N_HEADS_B)]

    lb = _lower_bound(lbp_ref)
    row = _iota2((c_len, c_len), 0)
    col = _iota2((c_len, c_len), 1)
    tri = (col <= row).astype(BF16)
    half_sizes = (32, 16, 8, 4, 2, 1)
    masks, picks = [], []
    for n in half_sizes:
        masks.append((_div2(row, 2 * n) == _div2(col, 2 * n)) & (_mod2(_div2(row, n), 2) == 1)
                     & (_mod2(_div2(col, n), 2) == 0))
        picks.append((col == _div2(row, 2 * n) * (2 * n) + (n - 1)).astype(BF16))
    pick_all = jnp.concatenate(picks, axis=0)
    on_diag = row == col

    for c, cs in enumerate(chunks):
        q, kk, g = _gates(qb_ref[cs, :], fb_ref[cs, :], lb)
        q_s[cs, :] = q
        kk_s[cs, :] = kk
        v_s[cs, :] = ib_ref[cs, :].astype(BF16)
        b_c = _dot2_lhs(tri, g)
        b_s[cs, :] = b_c
        refs = _dot(pick_all, b_c.astype(BF16))
        for li in range(len(half_sizes)):
            ref_s[li, cs, :] = refs[li * c_len:(li + 1) * c_len, :]
        b_end = b_c[c_len - 1:c_len, :]
        qin_s[cs, :] = (q * jnp.exp(b_c)).astype(BF16)
        kout_s[cs, :] = (kk * jnp.exp(b_end - b_c)).astype(BF16)
        qk = q * kk
        for h, hs in enumerate(heads):
            a_s[c, h] = jnp.where(on_diag, jnp.sum(qk[:, hs], axis=-1, keepdims=True), 0.0)
    for li, mask in enumerate(masks):
        for c, cs in enumerate(chunks):
            d = b_s[cs, :] - ref_s[li, cs, :]
            ql = (q_s[cs, :] * jnp.exp(d)).astype(BF16)
            kl = (kk_s[cs, :] * jnp.exp(-d)).astype(BF16)
            for h, hs in enumerate(heads):
                part = lax.dot_general(ql[:, hs], kl[:, hs], NT_DIMS, preferred_element_type=F32)
                a_s[c, h] = jnp.where(mask, part, a_s[c, h])

    for c, cs in enumerate(chunks):
        dec = jnp.exp(b_s[c * c_len + c_len - 1:(c + 1) * c_len, :])
        for h, hs in enumerate(heads):
            state = st_ref[h]
            inter = lax.dot_general(qin_s[cs, hs], state.astype(BF16), NT_DIMS, preferred_element_type=F32)
            intra = _dot(a_s[c, h].astype(BF16), v_s[cs, hs])
            o_ref[cs, hs] = (inter + intra).astype(o_ref.dtype)
            upd = lax.dot_general(v_s[cs, hs], kout_s[cs, hs], TN_DIMS, preferred_element_type=F32)
            st_ref[h] = state * dec[:, hs] + upd


def _hgrn_sample_body(qb_ref, fb_ref, ib_ref, lbp_ref, s0_ref, o_ref, s_ref, *, nbatch):
    rows_n = nbatch * DEC_SEQ
    assert rows_n == LANES
    lb = _lower_bound(lbp_ref)
    row = _iota2((rows_n, rows_n), 0)
    col = _iota2((rows_n, rows_n), 1)
    same = _div2(row, DEC_SEQ) == _div2(col, DEC_SEQ)
    tri = (same & (col <= row)).astype(BF16)
    pick = (col == row * DEC_SEQ + (DEC_SEQ - 1)).astype(BF16)

    q, kk, g = _gates(qb_ref[...], fb_ref[...], lb)
    v = ib_ref[...]
    b = _dot3(tri, g)
    b_end = _bcast_rows(b, DEC_SEQ, DEC_SEQ - 1)
    q_in = (q * jnp.exp(b)).astype(BF16)
    k_out = kk * jnp.exp(b_end - b)
    v16 = v.astype(BF16)

    a_heads = [jnp.zeros((rows_n, rows_n), F32) for _ in range(N_HEADS_B)]
    a_heads = _diag_block_scores(q, kk, b, a_heads, row, col, DEC_SEQ)

    rsel = _div2(_iota2((rows_n, 1), 0), DEC_SEQ)
    for h in range(N_HEADS_B):
        hs = slice(h * EXPAND_B, (h + 1) * EXPAND_B)
        intra = _dot(a_heads[h].astype(BF16), v16[:, hs])
        dec_t = jnp.exp(_dot3(pick, b[:, hs])).T
        k_out_t = k_out[:, hs].T.astype(BF16)
        for bi in range(nbatch):
            rs = slice(bi * DEC_SEQ, (bi + 1) * DEC_SEQ)
            s0 = s0_ref[bi, h]
            inter = _dot(q_in[rs, hs], s0.astype(BF16))
            o_ref[rs, hs] = inter + intra[rs, :]
            v_b = jnp.where(rsel == bi, v16[:, hs], jnp.zeros_like(v16[:, hs]))
            upd = _dot(k_out_t, v_b)
            s_ref[bi, h] = s0 * dec_t[:, bi:bi + 1] + upd


def _hgrn_sample(qb, fb, ib, lb_param, s0, nbatch):
    b = s0.shape[0]
    rows_n = nbatch * DEC_SEQ
    blk = pl.BlockSpec((rows_n, D_F), lambda i: (i, 0))
    sblk = pl.BlockSpec((nbatch, N_HEADS_B, EXPAND_B, HEAD_V_B), lambda i: (i, 0, 0, 0))
    return pl.pallas_call(
        functools.partial(_hgrn_sample_body, nbatch=nbatch),
        grid=(b // nbatch,),
        in_specs=[blk, blk, blk, pl.BlockSpec((DEPTH + 1, D_F), lambda i: (0, 0)), sblk],
        out_specs=[blk, sblk],
        out_shape=[jax.ShapeDtypeStruct((b * DEC_SEQ, D_B), F32),
                   jax.ShapeDtypeStruct(s0.shape, F32)],
        compiler_params=_cparams(("arbitrary",), 32),
        name="hgrn_sample",
    )(qb, fb, ib, lb_param, s0)


EXT_KEYS = WINDOW_MAX + LANES
SHIFT_ROWS = 64


def _sample_key_tables():
    t = np.arange(DEC_SEQ)[:, None]
    e = np.arange(EXT_KEYS)[None, :]
    is_new = e >= WINDOW_MAX
    d = np.where(is_new, t - (e - WINDOW_MAX), WINDOW_MAX + t - e)
    masks = np.full((len(DILATIONS), DEC_SEQ, EXT_KEYS), -np.inf, np.float32)
    for ci, dil in enumerate(DILATIONS):
        ok = (d >= 0) & (d % dil == 0) & (d <= WIN_STEPS * dil) & (e < WINDOW_MAX + DEC_SEQ)
        masks[ci][ok] = 0.0
    return np.maximum(d, 0), masks


def _window_shift(kn_ref, vn_ref, ck_ref, cv_ref, ok_ref, ov_ref):
    keep = LANES - DEC_SEQ
    lane = _iota2((SHIFT_ROWS, LANES), 1)
    pad_rows = jnp.zeros((keep, D_A), F32)
    for c_ref, n_ref, o_ref in ((ck_ref, kn_ref, ok_ref), (cv_ref, vn_ref, ov_ref)):
        new_t = jnp.concatenate([pad_rows, n_ref[...]], axis=0).T
        for r0 in range(0, D_A, SHIFT_ROWS):
            rows = slice(r0, r0 + SHIFT_ROWS)
            rolled = pltpu.roll(c_ref[rows, :], WINDOW_MAX - DEC_SEQ, axis=1)
            o_ref[rows, 0:WINDOW_MAX - LANES] = rolled[:, 0:WINDOW_MAX - LANES]
            o_ref[rows, WINDOW_MAX - LANES:] = jnp.where(lane >= keep, new_t[rows, :], rolled[:, WINDOW_MAX - LANES:])


def _sample_heads(q_ref, kn_ref, vn_ref, ck_ref, cv_ref, base_ref, mask_ref, attn_ref):
    keep = LANES - DEC_SEQ
    qs = q_ref[...] * ATTN_SCALE
    pad_new = jnp.zeros((keep, HEAD_DIM_A), F32)
    for h in range(N_HEADS_A):
        hs = slice(HEAD_DIM_A * h, HEAD_DIM_A * (h + 1))
        qh = qs[:, hs].astype(BF16)
        s_cache = _dot(qh, ck_ref[hs, :].astype(BF16))
        kn_pad = jnp.concatenate([kn_ref[:, hs], pad_new], axis=0).astype(BF16)
        s_new = lax.dot_general(qh, kn_pad, NT_DIMS, preferred_element_type=F32)
        s = jnp.concatenate([s_cache, s_new], axis=1) + base_ref[h]
        probs, lses = [], []
        for ci in range(len(DILATIONS)):
            sc = s + mask_ref[ci]
            m = jnp.max(sc, axis=1, keepdims=True)
            p = jnp.exp(sc - m)
            den = jnp.sum(p, axis=1, keepdims=True)
            probs.append(p / den)
            lses.append(m + jnp.log(den))
        lmax = jnp.maximum(jnp.maximum(lses[0], lses[1]), lses[2])
        ws = [jnp.exp(l - lmax) for l in lses]
        wsum = ws[0] + ws[1] + ws[2]
        pw = (probs[0] * (ws[0] / wsum) + probs[1] * (ws[1] / wsum) + probs[2] * (ws[2] / wsum)).astype(BF16)
        o = lax.dot_general(pw[:, 0:WINDOW_MAX], cv_ref[hs, :].astype(BF16), NT_DIMS, preferred_element_type=F32)
        vn_pad = jnp.concatenate([vn_ref[:, hs], pad_new], axis=0).astype(BF16)
        attn_ref[:, hs] = o + _dot(pw[:, WINDOW_MAX:], vn_pad)


N_ATTN_SAMPLE_IN = 7
N_ATTN_SAMPLE_OUT = 3


def _attn_sample_hgrn_prompt_body(*refs, n_chunks, tiles_per_seq):
    n_in = N_ATTN_SAMPLE_IN + 4
    n_out = N_ATTN_SAMPLE_OUT + 2
    ins, outs, scratch = refs[:n_in], refs[n_in:n_in + n_out], refs[n_in + n_out:]
    o_ref, s_ref = outs[N_ATTN_SAMPLE_OUT:]
    st_ref = scratch[0]
    t = lax.rem(pl.program_id(0), tiles_per_seq)

    @pl.when(t == 0)
    def _():
        st_ref[...] = jnp.zeros_like(st_ref)

    q_ref, kn_ref, vn_ref, ck_ref, cv_ref, base_ref, mask_ref = ins[:N_ATTN_SAMPLE_IN]
    attn_ref, ok_ref, ov_ref = outs[:N_ATTN_SAMPLE_OUT]
    _sample_heads(q_ref, kn_ref, vn_ref, ck_ref, cv_ref, base_ref, mask_ref, attn_ref)
    _hgrn_prompt_tile(*ins[N_ATTN_SAMPLE_IN:], o_ref, *scratch, n_chunks=n_chunks)
    _window_shift(kn_ref, vn_ref, ck_ref, cv_ref, ok_ref, ov_ref)

    @pl.when(t == tiles_per_seq - 1)
    def _():
        for h in range(N_HEADS_B):
            s_ref[h] = st_ref[h].T


def _attn_sample_hgrn_prompt(q, kn, vn, cache_kt, cache_vt, base, masks, qb, fb, ib, lb_param, bsz, seq):
    dec_b = cache_kt.shape[0]
    tt = bsz * seq // dec_b
    assert bsz * seq == tt * dec_b and seq % tt == 0 and tt % GLA_CHUNK == 0
    n_chunks = tt // GLA_CHUNK
    small = pl.BlockSpec((DEC_SEQ, D_A), lambda i: (i, 0))
    big = pl.BlockSpec((None, D_A, WINDOW_MAX), lambda i: (i, 0, 0))
    const = lambda a: pl.BlockSpec(a.shape, lambda i: (0,) * a.ndim)
    blk = pl.BlockSpec((tt, D_F), lambda i: (i, 0))
    state_blk = pl.BlockSpec((None, N_HEADS_B, EXPAND_B, HEAD_V_B), lambda i: (i // (seq // tt), 0, 0, 0))
    tile = lambda dt: pltpu.VMEM((tt, D_F), dt)
    return pl.pallas_call(
        functools.partial(_attn_sample_hgrn_prompt_body, n_chunks=n_chunks, tiles_per_seq=seq // tt),
        grid=(dec_b,),
        in_specs=[small, small, small, big, big, const(base), const(masks), blk, blk, blk, const(lb_param)],
        out_specs=[small, big, big, blk, state_blk],
        out_shape=[jax.ShapeDtypeStruct((dec_b * DEC_SEQ, D_A), F32),
                   jax.ShapeDtypeStruct(cache_kt.shape, F32),
                   jax.ShapeDtypeStruct(cache_vt.shape, F32),
                   jax.ShapeDtypeStruct((bsz * seq, D_B), BF16),
                   jax.ShapeDtypeStruct((bsz, N_HEADS_B, EXPAND_B, HEAD_V_B), F32)],
        scratch_shapes=[pltpu.VMEM((N_HEADS_B, HEAD_V_B, EXPAND_B), F32),
                        tile(F32), tile(F32), tile(F32),
                        pltpu.VMEM((6, tt, D_F), F32),
                        tile(BF16), tile(BF16), tile(BF16),
                        pltpu.VMEM((n_chunks, N_HEADS_B, GLA_CHUNK, GLA_CHUNK), F32)],
        compiler_params=_cparams(("arbitrary",), 56),
        name="attn_sample_hgrn_prompt",
    )(q, kn, vn, cache_kt, cache_vt, base, masks, qb, fb, ib, lb_param)


def _merge_body(*refs, n_cfg):
    x_ref = refs[0]
    pairs = [slice(LANES * j, LANES * (j + 1)) for j in range(N_PAIRS)]
    if n_cfg > 1:
        o_refs = refs[1:1 + n_cfg]
        l_refs = refs[1 + n_cfg:1 + 2 * n_cfg]
        tail = refs[1 + 2 * n_cfg:9 + 2 * n_cfg]
        scratch = refs[9 + 2 * n_cfg:]
        o_nat, l_nat = [], []
        for ci, dil in enumerate(DILATIONS):
            if dil == 1:
                blocks = range(PROJ_TILE // ATTN_BLOCK)
                o_val = jnp.concatenate([o_refs[ci][0, k] for k in blocks], axis=0).astype(F32)
                o_nat.append([o_val[:, lanes] for lanes in pairs])
                l_nat.append(jnp.concatenate([l_refs[ci][0, k] for k in blocks], axis=0))
            else:
                o_slabs, l_slab = scratch[2 * (ci - 1)], scratch[2 * (ci - 1) + 1]
                o_tmp, l_tmp = scratch[-2:]
                _load_blocked(o_refs[ci], o_slabs, dil, o_tmp)
                _load_blocked(l_refs[ci], l_slab, dil, l_tmp)
                o_nat.append([o_slabs[j] for j in range(N_PAIRS)])
                l_nat.append(l_slab[0])
        et_mat = _head_expand_matrix()
        lmax = functools.reduce(jnp.maximum, l_nat)
        ws = [jnp.exp(l - lmax) for l in l_nat]
        wsum = functools.reduce(lambda a, c: a + c, ws)
        attn = [None] * N_PAIRS
        for w, o_c in zip(ws, o_nat):
            w_lanes = _dot2_rhs(w / wsum, et_mat)
            for j, lanes in enumerate(pairs):
                term = w_lanes[:, lanes] * o_c[j]
                attn[j] = term if attn[j] is None else attn[j] + term
    else:
        tail = refs[2:10]
        attn = [refs[1][:, lanes].astype(F32) for lanes in pairs]
    ga_ref, ob_ref, gb_ref, wout_ref, ng_ref, lg_ref, lbias_ref, y_ref = tail

    parts = [(attn[j] * _silu(ga_ref[:, lanes].astype(F32))).astype(BF16) for j, lanes in enumerate(pairs)]
    for h in range(N_HEADS_B):
        hs = slice(h * HEAD_V_B, (h + 1) * HEAD_V_B)
        oh = ob_ref[:, hs].astype(F32)
        ms = jnp.mean(oh * oh, axis=-1, keepdims=True)
        on = oh * lax.rsqrt(ms + NORM_EPS) * ng_ref[...]
        parts.append((on * _silu(gb_ref[:, hs].astype(F32))).astype(BF16))
    mix = jnp.concatenate(parts, axis=-1)
    z = ALPHA * x_ref[...] + _dot(mix, wout_ref[...])
    mu = jnp.mean(z, axis=-1, keepdims=True)
    zc = z - mu
    var = jnp.mean(zc * zc, axis=-1, keepdims=True)
    y_ref[...] = zc * lax.rsqrt(var + NORM_EPS) * lg_ref[...] + lbias_ref[...]


def _merge(x2d, attn_outs, lses, ga, ob, gb, w_out_bf16, norm_g, ln_g, ln_b, tm, seq=None):
    n = x2d.shape[0]
    n_cfg = len(attn_outs)
    row = lambda width: pl.BlockSpec((tm, width), lambda i: (i, 0))
    const = lambda shape: pl.BlockSpec(shape, lambda i: (0,) * len(shape))
    if n_cfg > 1:
        assert tm == PROJ_TILE and n_cfg == len(DILATIONS)
        attn_specs = ([_blocked_tile_spec(seq, dil, D_A) for dil in DILATIONS]
                      + [_blocked_tile_spec(seq, dil, LANES) for dil in DILATIONS])
        scratch = []
        for dil in DILATIONS[1:] + (0,):
            scratch += [pltpu.VMEM((N_PAIRS, tm, LANES), F32), pltpu.VMEM((1, tm, LANES), F32)]
    else:
        attn_specs, scratch = [row(D_A)], []
    in_specs = ([row(D_MODEL)] + attn_specs
                + [row(D_A), row(D_B), row(D_B), const((D_MODEL, D_MODEL)),
                   const((1, HEAD_V_B)), const((1, D_MODEL)), const((1, D_MODEL))])
    return pl.pallas_call(
        functools.partial(_merge_body, n_cfg=n_cfg),
        grid=(n // tm,),
        in_specs=in_specs,
        out_specs=row(D_MODEL),
        out_shape=jax.ShapeDtypeStruct((n, D_MODEL), F32),
        scratch_shapes=scratch,
        compiler_params=_cparams(("arbitrary",), 48),
        name=f"merge_{n_cfg}",
    )(x2d, *attn_outs, *lses, ga, ob, gb, w_out_bf16, norm_g, ln_g, ln_b)


def _bias_by_distance(rel_bias, dist):
    onehot = jax.nn.one_hot(_rel_buckets(dist.reshape(-1)), NUM_BUCKETS, dtype=F32)
    table = jnp.dot(onehot, rel_bias.astype(F32), precision=lax.Precision.HIGHEST)
    return table.reshape(dist.shape + (N_HEADS_A,))


def _prompt_bias(rel_bias, dil):
    i = np.arange(ATTN_BLOCK)[:, None]
    j = np.arange(2 * ATTN_BLOCK)[None, :]
    sub = ATTN_BLOCK + i - j
    ok = (sub >= 0) & (sub <= WIN_STEPS)
    bias = _bias_by_distance(rel_bias, dil * np.clip(sub, 0, WIN_STEPS)) * LOG2_E
    full = jnp.where(ok[:, :, None], bias, -jnp.inf)
    no_prev = jnp.where((ok & (j >= ATTN_BLOCK))[:, :, None], bias, -jnp.inf)
    return jnp.transpose(jnp.concatenate([full, no_prev], axis=2), (2, 0, 1))


def _to_feature_major(cache):
    b, p, h, d = cache.shape
    return jnp.transpose(cache, (0, 2, 3, 1)).reshape(b, h * d, p)


def _from_feature_major(cache_t):
    b, _, p = cache_t.shape
    return jnp.transpose(cache_t.reshape(b, N_HEADS_A, HEAD_DIM_A, p), (0, 3, 1, 2))


def kernel(x_prompt, x_sample, cache_k, cache_v, state_hgrn, w_in, w_out, rel_bias, lb_param, hgrn_norm_g, ln_g, ln_b):
    bsz, seq, _ = x_prompt.shape
    dec_b, dec_t, _ = x_sample.shape
    assert dec_t == DEC_SEQ and cache_k.shape[2] == WINDOW_MAX and w_in.shape[0] == DEPTH
    keep = min(WINDOW_MAX, seq)

    w_in16 = w_in[0].astype(BF16)
    w_out16 = w_out[0].astype(BF16)
    norm_g = hgrn_norm_g[0].reshape(1, HEAD_V_B).astype(F32)
    lng = ln_g[0].reshape(1, D_MODEL).astype(F32)
    lnb = ln_b[0].reshape(1, D_MODEL).astype(F32)
    lbp = lb_param.astype(F32)

    tm = PROJ_TILE
    n_p = bsz * seq
    main = lambda i: (i, 0)
    o_qa, o_ka, o_va, o_ga, o_qb, o_fb, o_ib, o_gb = (D_A * 0, D_A * 1, D_A * 2, D_A * 3, 4 * D_A,
                                                    4 * D_A + D_F, 4 * D_A + 2 * D_F, 4 * D_A + 2 * D_F + D_B)
    x2d = x_prompt.reshape(n_p, D_MODEL)
    proj = _proj_prompt(x2d, w_in16, bsz, seq, keep)
    n_cfg = len(DILATIONS)
    qs, ks, vs = proj[0:n_cfg], proj[n_cfg:2 * n_cfg], proj[2 * n_cfg:3 * n_cfg]
    kwin, vwin, ga, qb, fb, ib, gb = proj[3 * n_cfg:]

    attn_outs, lses = [], []
    for ci, dil in enumerate(DILATIONS):
        o, lse = _attn_prompt(qs[ci], ks[ci], vs[ci], _prompt_bias(rel_bias, dil), dil)
        attn_outs.append(o)
        lses.append(lse)

    n_s = dec_b * dec_t
    outs_s = [(lo, D_A, 1.0, F32, n_s, main) for lo in (o_qa, o_ka, o_va, o_ga, o_qb, o_fb, o_ib, o_gb)]
    xs2d = x_sample.reshape(n_s, D_MODEL)
    tm_s = min(tm, n_s)
    sqa, ska, sva, sga, sqb, sfb, sib, sgb = _proj(xs2d, w_in16, outs_s, tm_s)
    dist_s, masks_s = _sample_key_tables()
    base_s = jnp.transpose(_bias_by_distance(rel_bias, dist_s), (2, 0, 1))
    attn_s, k_new_t, v_new_t, ob, s_prompt = _attn_sample_hgrn_prompt(
        sqa, ska, sva, _to_feature_major(cache_k[0]), _to_feature_major(cache_v[0]), base_s, jnp.asarray(masks_s),
        qb, fb, ib, lbp, bsz, seq)
    y_prompt = _merge(x2d, attn_outs, lses, ga, ob, gb, w_out16, norm_g, lng, lnb, tm, seq).reshape(bsz, seq, D_MODEL)
    ob_s, s_sample = _hgrn_sample(sqb, sfb, sib, lbp, state_hgrn[0], LANES // DEC_SEQ)
    y_sample = _merge(xs2d, [attn_s], [], sga, ob_s, sgb, w_out16, norm_g, lng, lnb, tm_s).reshape(dec_b, dec_t, D_MODEL)

    return (y_prompt, y_sample,
            _from_feature_major(kwin)[None], _from_feature_major(vwin)[None],
            _from_feature_major(k_new_t)[None], _from_feature_major(v_new_t)[None],
            s_prompt[None], s_sample[None].astype(state_hgrn.dtype))
```

```python
import functools

import numpy as np
import jax
import jax.numpy as jnp
from jax import lax
from jax.experimental import pallas as pl
from jax.experimental.pallas import tpu as pltpu

F32 = jnp.float32
BF16 = jnp.bfloat16

D_MODEL = 1024
D_A = 512
HEAD_DIM_A = 64
N_HEADS_A = 8
DILATIONS = (1, 4, 16)
WIN_STEPS = 128
WINDOW_MAX = 2048
ATTN_BLOCK = 128
ATTN_SCALE = HEAD_DIM_A ** -0.5
NUM_BUCKETS = 32
REL_MAX_DIST = 2048
D_B = 512
N_HEADS_B = 4
HEAD_V_B = 128
EXPAND_B = 128
D_F = 512
GLA_CHUNK = 64
D_IN = 4 * D_A + 2 * D_F + 2 * D_B
DEPTH = 1
ALPHA = (2.0 * DEPTH) ** 0.25
NORM_EPS = 1e-5
DEC_SEQ = 8
LOG2_E = 1.4426950408889634
LN_2 = 0.6931471805599453

LANES = 128
MIB = 1024 * 1024
VMEM_LARGE_MIB = 56
VMEM_MERGE_MIB = 48
VMEM_ATTN_MIB = 40
VMEM_SMALL_MIB = 32

NT_DIMS = (((1,), (1,)), ((), ()))
TN_DIMS = (((0,), (0,)), ((), ()))


def _cparams(semantics, vmem_mib):
    return pltpu.CompilerParams(dimension_semantics=semantics, vmem_limit_bytes=vmem_mib * MIB)


def _rel_buckets(dist):
    max_exact = NUM_BUCKETS // 2
    d = np.maximum(dist, 1).astype(np.float32)
    large = max_exact + (np.log(d / max_exact) / np.log(REL_MAX_DIST / max_exact)
                         * (NUM_BUCKETS - max_exact)).astype(np.int32)
    large = np.minimum(large, NUM_BUCKETS - 1)
    return np.where(dist < max_exact, dist, large).astype(np.int32)


def _split3(x):
    hi = x.astype(BF16)
    r1 = x - hi.astype(F32)
    mid = r1.astype(BF16)
    lo = (r1 - mid.astype(F32)).astype(BF16)
    return hi, mid, lo


def _dot(a, b):
    return jnp.dot(a, b, preferred_element_type=F32)


def _dot3(mat_bf16, x):
    hi, mid, lo = _split3(x)
    return _dot(mat_bf16, hi) + _dot(mat_bf16, mid) + _dot(mat_bf16, lo)


def _split2(x):
    hi = x.astype(BF16)
    return hi, (x - hi.astype(F32)).astype(BF16)


def _dot2_lhs(mat_bf16, x):
    hi, lo = _split2(x)
    return _dot(mat_bf16, hi) + _dot(mat_bf16, lo)


def _sigmoid(x):
    return 0.5 * jnp.tanh(0.5 * x) + 0.5


def _silu(x):
    return x * _sigmoid(x)


def _bcast_rows(x, block, j):
    rows = x.shape[0]
    parts = [jnp.broadcast_to(x[r0 + j:r0 + j + 1, :], (block, x.shape[1])) for r0 in range(0, rows, block)]
    return parts[0] if len(parts) == 1 else jnp.concatenate(parts, axis=0)


def _proj_body(x_ref, w_ref, *out_refs, widths):
    x = x_ref[...].astype(BF16)
    lo = 0
    for ref, width in zip(out_refs, widths):
        ref[...] = _dot(x, w_ref[:, lo:lo + width])
        lo += width


def _proj(x2d, w_bf16, widths, tm):
    n = x2d.shape[0]
    return pl.pallas_call(
        functools.partial(_proj_body, widths=widths),
        grid=(n // tm,),
        in_specs=[pl.BlockSpec((tm, D_MODEL), lambda i: (i, 0)),
                  pl.BlockSpec((D_MODEL, D_IN), lambda i: (0, 0))],
        out_specs=[pl.BlockSpec((tm, width), lambda i: (i, 0)) for width in widths],
        out_shape=[jax.ShapeDtypeStruct((n, width), F32) for width in widths],
        compiler_params=_cparams(("arbitrary",), VMEM_LARGE_MIB),
        name="proj",
    )(x2d, w_bf16)


PROJ_TILE = 512
N_PAIRS = D_A // LANES


def _blocked_shape(bsz, seq, dil, width, dtype):
    return jax.ShapeDtypeStruct((bsz, dil, seq // (dil * ATTN_BLOCK), ATTN_BLOCK, width), dtype)


def _blocked_tile_spec(seq, dil, width):
    tiles = seq // PROJ_TILE
    rows = PROJ_TILE // dil
    if rows >= ATTN_BLOCK:
        return pl.BlockSpec((None, dil, rows // ATTN_BLOCK, ATTN_BLOCK, width),
                            lambda i: (i // tiles, 0, i % tiles, 0, 0))
    per = ATTN_BLOCK // rows
    return pl.BlockSpec((None, dil, None, rows, width),
                        lambda i: (i // tiles, 0, (i % tiles) // per, (i % tiles) % per, 0))


def _store_blocked(out_refs, slabs, slabs4):
    assert DILATIONS == (1, 4, 16) and PROJ_TILE == 4 * ATTN_BLOCK
    out4, out16 = out_refs
    sub = ATTN_BLOCK // 4
    for j in range(slabs.shape[0]):
        lanes = slice(LANES * j, LANES * (j + 1))
        for c in range(4):
            piece = slabs[j, pl.ds(c, ATTN_BLOCK, stride=4), :]
            slabs4[j, c * ATTN_BLOCK:(c + 1) * ATTN_BLOCK, :] = piece
            out4[c, 0, :, lanes] = piece.astype(out4.dtype)
        for c in range(4):
            for r in range(4):
                piece = slabs4[j, pl.ds(c * ATTN_BLOCK + r, sub, stride=4), :]
                out16[c + 4 * r, :, lanes] = piece.astype(out16.dtype)


def _load_blocked(in_ref, slabs, dil, slabs4=None):
    sub = ATTN_BLOCK // 4
    for j in range(slabs.shape[0]):
        lanes = slice(LANES * j, LANES * (j + 1))
        for c in range(4):
            if dil == 4:
                piece = in_ref[c, 0, :, lanes].astype(F32)
            else:
                for r in range(4):
                    slabs4[j, pl.ds(c * ATTN_BLOCK + r, sub, stride=4), :] = in_ref[c + 4 * r, :, lanes].astype(F32)
                piece = slabs4[j, c * ATTN_BLOCK:(c + 1) * ATTN_BLOCK, :]
            slabs[j, pl.ds(c, ATTN_BLOCK, stride=4), :] = piece


def _proj_prompt_body(x_ref, w_ref, *refs):
    n_cfg = len(DILATIONS)
    blocked = [refs[n_cfg * g:n_cfg * (g + 1)] for g in range(3)]
    kwin_ref, vwin_ref, ga_ref, qb_ref, fb_ref, ib_ref, gb_ref = refs[3 * n_cfg:3 * n_cfg + 7]
    x16_ref = refs[3 * n_cfg + 7]
    slab_sets = refs[3 * n_cfg + 8:3 * n_cfg + 11]
    slab4_sets = refs[3 * n_cfg + 11:]
    x16_ref[...] = x_ref[...].astype(BF16)
    for g, scale in enumerate((ATTN_SCALE * LOG2_E, 1.0, 1.0)):
        h = _dot(x16_ref[...], w_ref[:, D_A * g:D_A * (g + 1)])
        if scale != 1.0:
            h = h * scale
        for j in range(N_PAIRS):
            slab_sets[g][j] = h[:, LANES * j:LANES * (j + 1)]
        for k in range(PROJ_TILE // ATTN_BLOCK):
            blocked[g][0][0, k] = h[k * ATTN_BLOCK:(k + 1) * ATTN_BLOCK, :].astype(BF16)
    lo = 3 * D_A
    for ref, width in ((ga_ref, D_A), (qb_ref, D_F), (fb_ref, D_F), (ib_ref, D_B), (gb_ref, D_B)):
        ref[...] = _dot(x16_ref[...], w_ref[:, lo:lo + width]).astype(ref.dtype)
        lo += width
    for g, win_ref in enumerate((None, kwin_ref, vwin_ref)):
        if win_ref is not None:
            for j in range(N_PAIRS):
                win_ref[LANES * j:LANES * (j + 1), :] = slab_sets[g][j].T
        _store_blocked(blocked[g][1:], slab_sets[g], slab4_sets[g])


def _proj_prompt(x2d, w_bf16, bsz, seq, keep):
    n = bsz * seq
    tiles = seq // PROJ_TILE
    win_tiles = keep // PROJ_TILE
    row = lambda width: pl.BlockSpec((PROJ_TILE, width), lambda i: (i, 0))
    win = pl.BlockSpec((None, D_A, PROJ_TILE),
                       lambda i: (i // tiles, 0, jnp.maximum(i % tiles - (tiles - win_tiles), 0)))
    out_shape, out_specs = [], []
    for _ in range(3):
        for dil in DILATIONS:
            out_shape.append(_blocked_shape(bsz, seq, dil, D_A, BF16))
            out_specs.append(_blocked_tile_spec(seq, dil, D_A))
    for _ in range(2):
        out_shape.append(jax.ShapeDtypeStruct((bsz, D_A, keep), F32))
        out_specs.append(win)
    for width, dt in ((D_A, BF16), (D_F, F32), (D_F, F32), (D_B, F32), (D_B, BF16)):
        out_shape.append(jax.ShapeDtypeStruct((n, width), dt))
        out_specs.append(row(width))
    return pl.pallas_call(
        _proj_prompt_body,
        grid=(n // PROJ_TILE,),
        in_specs=[row(D_MODEL), pl.BlockSpec((D_MODEL, D_IN), lambda i: (0, 0))],
        out_specs=out_specs,
        out_shape=out_shape,
        scratch_shapes=([pltpu.VMEM((PROJ_TILE, D_MODEL), BF16)]
                        + [pltpu.VMEM((N_PAIRS, PROJ_TILE, LANES), F32) for _ in range(6)]),
        compiler_params=_cparams(("arbitrary",), VMEM_LARGE_MIB),
        name="proj_prompt",
    )(x2d, w_bf16)


UNITS_PER_STEP = 16


def _attn_unit(q, k_prev, k_cur, v_prev, v_cur, bias_ref, bias_set):
    lane = _iota2((ATTN_BLOCK, LANES), 1)
    low_half = lane < HEAD_DIM_A
    scores = []
    for j in range(N_PAIRS):
        lanes = slice(LANES * j, LANES * (j + 1))
        qj = q[:, lanes]
        zero = jnp.zeros_like(qj)
        kcat = jnp.concatenate([k_prev[:, lanes], k_cur[:, lanes]], axis=0)
        for qh in (jnp.where(low_half, qj, zero), jnp.where(low_half, zero, qj)):
            scores.append(lax.dot_general(qh, kcat, NT_DIMS, preferred_element_type=F32))

    probs, den_pairs = [], []
    m_tile = jnp.zeros((ATTN_BLOCK, LANES), F32)
    den_tile = jnp.ones((ATTN_BLOCK, LANES), F32)
    for h in range(N_HEADS_A):
        s = scores[h] + bias_ref[bias_set + h]
        m = jnp.max(jnp.maximum(s[:, :ATTN_BLOCK], s[:, ATTN_BLOCK:]), axis=1, keepdims=True)
        p = jnp.exp2(s - m)
        den = jnp.sum(p[:, :ATTN_BLOCK] + p[:, ATTN_BLOCK:], axis=1, keepdims=True)
        probs.append(p.astype(BF16))
        m_tile = jnp.where(lane == h, m, m_tile)
        den_tile = jnp.where(lane == h, den, den_tile)
        if h % 2 == 0:
            den_even = den
        else:
            den_pairs.append(jnp.where(low_half, den_even, den))
    lse_tile = (m_tile + jnp.log2(den_tile)) * LN_2

    outs = []
    for j in range(N_PAIRS):
        lanes = slice(LANES * j, LANES * (j + 1))
        vcat = jnp.concatenate([v_prev[:, lanes], v_cur[:, lanes]], axis=0)
        o_pair = jnp.where(low_half, _dot(probs[2 * j], vcat), _dot(probs[2 * j + 1], vcat))
        outs.append(o_pair / den_pairs[j])
    return outs, lse_tile


def _attn_prompt_body(q_ref, k_ref, v_ref, bias_ref, o_ref, lse_ref, kc_ref, vc_ref, *, n_res, n_blk):
    step = pl.program_id(2)
    slot = lax.rem(step, 2)

    @pl.when(step == 0)
    def _():
        kc_ref[0] = jnp.zeros_like(kc_ref[0])
        vc_ref[0] = jnp.zeros_like(vc_ref[0])

    def run(rr, n, k_prev, v_prev, bias_set):
        outs, lse_tile = _attn_unit(q_ref[rr, n], k_prev, k_ref[rr, n], v_prev, v_ref[rr, n], bias_ref, bias_set)
        for j in range(N_PAIRS):
            o_ref[rr, n, :, LANES * j:LANES * (j + 1)] = outs[j].astype(o_ref.dtype)
        lse_ref[rr, n] = lse_tile

    start_set = jnp.where(step == 0, N_HEADS_A, 0)
    for rr in range(n_res):
        run(rr, 0, kc_ref[slot, rr], vc_ref[slot, rr], start_set)

        def later(n, carry, rr=rr):
            run(rr, n, k_ref[rr, n - 1], v_ref[rr, n - 1], 0)
            return carry

        lax.fori_loop(1, n_blk, later, 0, unroll=True)
        kc_ref[1 - slot, rr] = k_ref[rr, n_blk - 1]
        vc_ref[1 - slot, rr] = v_ref[rr, n_blk - 1]


def _attn_prompt(q, k, v, bias, dil):
    b, _, nb, _, _ = q.shape
    n_blk = min(nb, UNITS_PER_STEP)
    n_res = min(dil, UNITS_PER_STEP // n_blk)
    blk = lambda width: pl.BlockSpec((None, n_res, n_blk, ATTN_BLOCK, width), lambda bi, ri, si: (bi, ri, si, 0, 0))
    return pl.pallas_call(
        functools.partial(_attn_prompt_body, n_res=n_res, n_blk=n_blk),
        grid=(b, dil // n_res, nb // n_blk),
        in_specs=[blk(D_A), blk(D_A), blk(D_A),
                  pl.BlockSpec((2 * N_HEADS_A, ATTN_BLOCK, 2 * ATTN_BLOCK), lambda bi, ri, si: (0, 0, 0))],
        out_specs=[blk(D_A), blk(LANES)],
        out_shape=[jax.ShapeDtypeStruct(q.shape, BF16),
                   jax.ShapeDtypeStruct(q.shape[:-1] + (LANES,), F32)],
        scratch_shapes=[pltpu.VMEM((2, n_res, ATTN_BLOCK, D_A), BF16), pltpu.VMEM((2, n_res, ATTN_BLOCK, D_A), BF16)],
        compiler_params=_cparams(("arbitrary", "arbitrary", "arbitrary"), VMEM_ATTN_MIB),
        name=f"attn_prompt_d{dil}",
    )(q, k, v, bias)


def _lower_bound(lbp_ref):
    p0 = lbp_ref[0:1, :]
    p1 = lbp_ref[1:2, :]
    m = jnp.maximum(p0, p1)
    e0 = jnp.exp(p0 - m)
    e1 = jnp.exp(p1 - m)
    return e0 / (e0 + e1)


def _gates(qb, fb, lb):
    f = lb + (1.0 - lb) * _sigmoid(fb)
    return _silu(qb) * (EXPAND_B ** -0.5), 1.0 - f, jnp.log(f)


def _iota2(shape, dim):
    return lax.broadcasted_iota(jnp.int32, shape, dim)


def _div2(x, n):
    return jnp.right_shift(x, int(n).bit_length() - 1)


def _mod2(x, n):
    return jnp.bitwise_and(x, n - 1)


def _head_expand_matrix():
    return (_iota2((LANES, D_A), 0) == _div2(_iota2((LANES, D_A), 1), HEAD_DIM_A)).astype(BF16)


def _diag_block_scores(q, kk, b, a_heads, row, col, block):
    same = _div2(row, block) == _div2(col, block)
    for j in range(block):
        bj = _bcast_rows(b, block, j)
        kj = _bcast_rows(kk, block, j)
        x = q * jnp.exp(b - bj) * kj
        sel = same & (_mod2(col, block) == j) & (_mod2(row, block) >= j)
        for h in range(N_HEADS_B):
            cs = jnp.sum(x[:, h * EXPAND_B:(h + 1) * EXPAND_B], axis=-1, keepdims=True)
            a_heads[h] = jnp.where(sel, cs, a_heads[h])
    return a_heads


def _hgrn_prompt_tile(qb_ref, fb_ref, ib_ref, lbp_ref, o_ref, st_ref, q_s, kk_s, b_s, ref_s, qin_s, kout_s, v_s, a_s,
                      *, n_chunks):
    c_len = GLA_CHUNK
    chunks = [slice(c * c_len, (c + 1) * c_len) for c in range(n_chunks)]
    heads = [slice(h * EXPAND_B, (h + 1) * EXPAND_B) for h in range(N_HEADS_B)]

    lb = _lower_bound(lbp_ref)
    row = _iota2((c_len, c_len), 0)
    col = _iota2((c_len, c_len), 1)
    tri = (col <= row).astype(BF16)
    half_sizes = (32, 16, 8, 4, 2, 1)
    masks, picks = [], []
    for n in half_sizes:
        masks.append((_div2(row, 2 * n) == _div2(col, 2 * n)) & (_mod2(_div2(row, n), 2) == 1)
                     & (_mod2(_div2(col, n), 2) == 0))
        picks.append((col == _div2(row, 2 * n) * (2 * n) + (n - 1)).astype(BF16))
    pick_all = jnp.concatenate(picks, axis=0)
    on_diag = row == col

    for c, cs in enumerate(chunks):
        q, kk, g = _gates(qb_ref[cs, :], fb_ref[cs, :], lb)
        q_s[cs, :] = q
        kk_s[cs, :] = kk
        v_s[cs, :] = ib_ref[cs, :].astype(BF16)
        b_c = _dot2_lhs(tri, g)
        b_s[cs, :] = b_c
        refs = _dot(pick_all, b_c.astype(BF16))
        for li in range(len(half_sizes)):
            ref_s[li, cs, :] = refs[li * c_len:(li + 1) * c_len, :]
        b_end = b_c[c_len - 1:c_len, :]
        qin_s[cs, :] = (q * jnp.exp(b_c)).astype(BF16)
        kout_s[cs, :] = (kk * jnp.exp(b_end - b_c)).astype(BF16)
        qk = q * kk
        for h, hs in enumerate(heads):
            a_s[c, h] = jnp.where(on_diag, jnp.sum(qk[:, hs], axis=-1, keepdims=True), 0.0)
    for li, mask in enumerate(masks):
        for c, cs in enumerate(chunks):
            d = b_s[cs, :] - ref_s[li, cs, :]
            ql = (q_s[cs, :] * jnp.exp(d)).astype(BF16)
            kl = (kk_s[cs, :] * jnp.exp(-d)).astype(BF16)
            for h, hs in enumerate(heads):
                part = lax.dot_general(ql[:, hs], kl[:, hs], NT_DIMS, preferred_element_type=F32)
                a_s[c, h] = jnp.where(mask, part, a_s[c, h])

    for c, cs in enumerate(chunks):
        dec = jnp.exp(b_s[c * c_len + c_len - 1:(c + 1) * c_len, :])
        for h, hs in enumerate(heads):
            state = st_ref[h]
            inter = lax.dot_general(qin_s[cs, hs], state.astype(BF16), NT_DIMS, preferred_element_type=F32)
            intra = _dot(a_s[c, h].astype(BF16), v_s[cs, hs])
            o_ref[cs, hs] = (inter + intra).astype(o_ref.dtype)
            upd = lax.dot_general(v_s[cs, hs], kout_s[cs, hs], TN_DIMS, preferred_element_type=F32)
            st_ref[h] = state * dec[:, hs] + upd


def _hgrn_sample_body(qb_ref, fb_ref, ib_ref, lbp_ref, s0_ref, o_ref, s_ref, *, nbatch):
    rows_n = nbatch * DEC_SEQ
    assert rows_n == LANES
    lb = _lower_bound(lbp_ref)
    row = _iota2((rows_n, rows_n), 0)
    col = _iota2((rows_n, rows_n), 1)
    same = _div2(row, DEC_SEQ) == _div2(col, DEC_SEQ)
    tri = (same & (col <= row)).astype(BF16)
    pick = (col == row * DEC_SEQ + (DEC_SEQ - 1)).astype(BF16)

    q, kk, g = _gates(qb_ref[...], fb_ref[...], lb)
    v = ib_ref[...]
    b = _dot3(tri, g)
    b_end = _bcast_rows(b, DEC_SEQ, DEC_SEQ - 1)
    q_in = (q * jnp.exp(b)).astype(BF16)
    k_out = kk * jnp.exp(b_end - b)
    v16 = v.astype(BF16)

    a_heads = [jnp.zeros((rows_n, rows_n), F32) for _ in range(N_HEADS_B)]
    a_heads = _diag_block_scores(q, kk, b, a_heads, row, col, DEC_SEQ)

    rsel = _div2(_iota2((rows_n, 1), 0), DEC_SEQ)
    for h in range(N_HEADS_B):
        hs = slice(h * EXPAND_B, (h + 1) * EXPAND_B)
        intra = _dot(a_heads[h].astype(BF16), v16[:, hs])
        dec_t = jnp.exp(_dot3(pick, b[:, hs])).T
        k_out_t = k_out[:, hs].T.astype(BF16)
        for bi in range(nbatch):
            rs = slice(bi * DEC_SEQ, (bi + 1) * DEC_SEQ)
            s0 = s0_ref[bi, h]
            inter = _dot(q_in[rs, hs], s0.astype(BF16))
            o_ref[rs, hs] = inter + intra[rs, :]
            v_b = jnp.where(rsel == bi, v16[:, hs], jnp.zeros_like(v16[:, hs]))
            upd = _dot(k_out_t, v_b)
            s_ref[bi, h] = s0 * dec_t[:, bi:bi + 1] + upd


def _hgrn_sample(qb, fb, ib, lb_param, s0, nbatch):
    b = s0.shape[0]
    rows_n = nbatch * DEC_SEQ
    blk = pl.BlockSpec((rows_n, D_F), lambda i: (i, 0))
    sblk = pl.BlockSpec((nbatch, N_HEADS_B, EXPAND_B, HEAD_V_B), lambda i: (i, 0, 0, 0))
    return pl.pallas_call(
        functools.partial(_hgrn_sample_body, nbatch=nbatch),
        grid=(b // nbatch,),
        in_specs=[blk, blk, blk, pl.BlockSpec((DEPTH + 1, D_F), lambda i: (0, 0)), sblk],
        out_specs=[blk, sblk],
        out_shape=[jax.ShapeDtypeStruct((b * DEC_SEQ, D_B), F32),
                   jax.ShapeDtypeStruct(s0.shape, F32)],
        compiler_params=_cparams(("arbitrary",), VMEM_SMALL_MIB),
        name="hgrn_sample",
    )(qb, fb, ib, lb_param, s0)


EXT_KEYS = WINDOW_MAX + LANES
SHIFT_ROWS = 64


def _sample_key_tables():
    t = np.arange(DEC_SEQ)[:, None]
    e = np.arange(EXT_KEYS)[None, :]
    is_new = e >= WINDOW_MAX
    d = np.where(is_new, t - (e - WINDOW_MAX), WINDOW_MAX + t - e)
    masks = np.full((len(DILATIONS), DEC_SEQ, EXT_KEYS), -np.inf, np.float32)
    for ci, dil in enumerate(DILATIONS):
        ok = (d >= 0) & (d % dil == 0) & (d <= WIN_STEPS * dil) & (e < WINDOW_MAX + DEC_SEQ)
        masks[ci][ok] = 0.0
    return np.maximum(d, 0), masks


def _window_shift(kn_ref, vn_ref, ck_ref, cv_ref, ok_ref, ov_ref):
    keep = LANES - DEC_SEQ
    lane = _iota2((SHIFT_ROWS, LANES), 1)
    pad_rows = jnp.zeros((keep, D_A), F32)
    for c_ref, n_ref, o_ref in ((ck_ref, kn_ref, ok_ref), (cv_ref, vn_ref, ov_ref)):
        new_t = jnp.concatenate([pad_rows, n_ref[...]], axis=0).T
        for r0 in range(0, D_A, SHIFT_ROWS):
            rows = slice(r0, r0 + SHIFT_ROWS)
            rolled = pltpu.roll(c_ref[rows, :], WINDOW_MAX - DEC_SEQ, axis=1)
            o_ref[rows, 0:WINDOW_MAX - LANES] = rolled[:, 0:WINDOW_MAX - LANES]
            o_ref[rows, WINDOW_MAX - LANES:] = jnp.where(lane >= keep, new_t[rows, :], rolled[:, WINDOW_MAX - LANES:])


def _sample_heads(q_ref, kn_ref, vn_ref, ck_ref, cv_ref, base_ref, mask_ref, attn_ref):
    keep = LANES - DEC_SEQ
    rows = N_HEADS_A * DEC_SEQ
    own = _div2(_iota2((rows, D_A), 0), DEC_SEQ) == _div2(_iota2((rows, D_A), 1), HEAD_DIM_A)
    qs = q_ref[...] * ATTN_SCALE
    q_all = jnp.where(own, jnp.concatenate([qs] * N_HEADS_A, axis=0), 0.0).astype(BF16)
    pad_new = jnp.zeros((keep, D_A), F32)
    kn_pad = jnp.concatenate([kn_ref[...], pad_new], axis=0).astype(BF16)
    vn_pad = jnp.concatenate([vn_ref[...], pad_new], axis=0).astype(BF16)
    s_cache = _dot(q_all, ck_ref[...].astype(BF16))
    s_new = lax.dot_general(q_all, kn_pad, NT_DIMS, preferred_element_type=F32)
    s_all = jnp.concatenate([s_cache, s_new], axis=1)
    pws = []
    for h in range(N_HEADS_A):
        s = s_all[DEC_SEQ * h:DEC_SEQ * (h + 1), :] + base_ref[h]
        probs, lses = [], []
        for ci in range(len(DILATIONS)):
            sc = s + mask_ref[ci]
            m = jnp.max(sc, axis=1, keepdims=True)
            p = jnp.exp(sc - m)
            den = jnp.sum(p, axis=1, keepdims=True)
            probs.append(p / den)
            lses.append(m + jnp.log(den))
        lmax = jnp.maximum(jnp.maximum(lses[0], lses[1]), lses[2])
        ws = [jnp.exp(l - lmax) for l in lses]
        wsum = ws[0] + ws[1] + ws[2]
        pws.append(probs[0] * (ws[0] / wsum) + probs[1] * (ws[1] / wsum) + probs[2] * (ws[2] / wsum))
    pw = jnp.concatenate(pws, axis=0).astype(BF16)
    o = lax.dot_general(pw[:, 0:WINDOW_MAX], cv_ref[...].astype(BF16), NT_DIMS, preferred_element_type=F32)
    o = jnp.where(own, o + _dot(pw[:, WINDOW_MAX:], vn_pad), 0.0)
    acc = o[0:DEC_SEQ, :]
    for h in range(1, N_HEADS_A):
        acc = acc + o[DEC_SEQ * h:DEC_SEQ * (h + 1), :]
    attn_ref[...] = acc


N_ATTN_SAMPLE_IN = 7
N_ATTN_SAMPLE_OUT = 3


def _attn_sample_hgrn_prompt_body(*refs, n_chunks, tiles_per_seq):
    n_in = N_ATTN_SAMPLE_IN + 4
    n_out = N_ATTN_SAMPLE_OUT + 2
    ins, outs, scratch = refs[:n_in], refs[n_in:n_in + n_out], refs[n_in + n_out:]
    o_ref, s_ref = outs[N_ATTN_SAMPLE_OUT:]
    st_ref = scratch[0]
    t = lax.rem(pl.program_id(0), tiles_per_seq)

    @pl.when(t == 0)
    def _():
        st_ref[...] = jnp.zeros_like(st_ref)

    q_ref, kn_ref, vn_ref, ck_ref, cv_ref, base_ref, mask_ref = ins[:N_ATTN_SAMPLE_IN]
    attn_ref, ok_ref, ov_ref = outs[:N_ATTN_SAMPLE_OUT]
    _sample_heads(q_ref, kn_ref, vn_ref, ck_ref, cv_ref, base_ref, mask_ref, attn_ref)
    _hgrn_prompt_tile(*ins[N_ATTN_SAMPLE_IN:], o_ref, *scratch, n_chunks=n_chunks)
    _window_shift(kn_ref, vn_ref, ck_ref, cv_ref, ok_ref, ov_ref)

    @pl.when(t == tiles_per_seq - 1)
    def _():
        for h in range(N_HEADS_B):
            s_ref[h] = st_ref[h].T


def _attn_sample_hgrn_prompt(q, kn, vn, cache_kt, cache_vt, base, masks, qb, fb, ib, lb_param, bsz, seq):
    dec_b = cache_kt.shape[0]
    tt = bsz * seq // dec_b
    assert bsz * seq == tt * dec_b and seq % tt == 0 and tt % GLA_CHUNK == 0
    n_chunks = tt // GLA_CHUNK
    small = pl.BlockSpec((DEC_SEQ, D_A), lambda i: (i, 0))
    big = pl.BlockSpec((None, D_A, WINDOW_MAX), lambda i: (i, 0, 0))
    const = lambda a: pl.BlockSpec(a.shape, lambda i: (0,) * a.ndim)
    blk = pl.BlockSpec((tt, D_F), lambda i: (i, 0))
    state_blk = pl.BlockSpec((None, N_HEADS_B, EXPAND_B, HEAD_V_B), lambda i: (i // (seq // tt), 0, 0, 0))
    tile = lambda dt: pltpu.VMEM((tt, D_F), dt)
    return pl.pallas_call(
        functools.partial(_attn_sample_hgrn_prompt_body, n_chunks=n_chunks, tiles_per_seq=seq // tt),
        grid=(dec_b,),
        in_specs=[small, small, small, big, big, const(base), const(masks), blk, blk, blk, const(lb_param)],
        out_specs=[small, big, big, blk, state_blk],
        out_shape=[jax.ShapeDtypeStruct((dec_b * DEC_SEQ, D_A), F32),
                   jax.ShapeDtypeStruct(cache_kt.shape, F32),
                   jax.ShapeDtypeStruct(cache_vt.shape, F32),
                   jax.ShapeDtypeStruct((bsz * seq, D_B), BF16),
                   jax.ShapeDtypeStruct((bsz, N_HEADS_B, EXPAND_B, HEAD_V_B), F32)],
        scratch_shapes=[pltpu.VMEM((N_HEADS_B, HEAD_V_B, EXPAND_B), F32),
                        tile(F32), tile(F32), tile(F32),
                        pltpu.VMEM((6, tt, D_F), F32),
                        tile(BF16), tile(BF16), tile(BF16),
                        pltpu.VMEM((n_chunks, N_HEADS_B, GLA_CHUNK, GLA_CHUNK), F32)],
        compiler_params=_cparams(("arbitrary",), VMEM_LARGE_MIB),
        name="attn_sample_hgrn_prompt",
    )(q, kn, vn, cache_kt, cache_vt, base, masks, qb, fb, ib, lb_param)


def _merge_body(*refs, n_cfg):
    x_ref = refs[0]
    pairs = [slice(LANES * j, LANES * (j + 1)) for j in range(N_PAIRS)]
    if n_cfg > 1:
        o_refs = refs[1:1 + n_cfg]
        l_refs = refs[1 + n_cfg:1 + 2 * n_cfg]
        tail = refs[1 + 2 * n_cfg:9 + 2 * n_cfg]
        scratch = refs[9 + 2 * n_cfg:]
        o_nat, l_nat = [], []
        for ci, dil in enumerate(DILATIONS):
            if dil == 1:
                blocks = range(PROJ_TILE // ATTN_BLOCK)
                o_val = jnp.concatenate([o_refs[ci][0, k] for k in blocks], axis=0).astype(F32)
                o_nat.append([o_val[:, lanes] for lanes in pairs])
                l_nat.append(jnp.concatenate([l_refs[ci][0, k] for k in blocks], axis=0))
            else:
                o_slabs, l_slab = scratch[2 * (ci - 1)], scratch[2 * (ci - 1) + 1]
                o_tmp, l_tmp = scratch[-2:]
                _load_blocked(o_refs[ci], o_slabs, dil, o_tmp)
                _load_blocked(l_refs[ci], l_slab, dil, l_tmp)
                o_nat.append([o_slabs[j] for j in range(N_PAIRS)])
                l_nat.append(l_slab[0])
        et_mat = _head_expand_matrix()
        lmax = functools.reduce(jnp.maximum, l_nat)
        ws = [jnp.exp(l - lmax) for l in l_nat]
        wsum = functools.reduce(lambda a, c: a + c, ws)
        attn = [None] * N_PAIRS
        for w, o_c in zip(ws, o_nat):
            w_lanes = _dot((w / wsum).astype(BF16), et_mat)
            for j, lanes in enumerate(pairs):
                term = w_lanes[:, lanes] * o_c[j]
                attn[j] = term if attn[j] is None else attn[j] + term
    else:
        tail = refs[2:10]
        attn = [refs[1][:, lanes].astype(F32) for lanes in pairs]
    ga_ref, ob_ref, gb_ref, wout_ref, ng_ref, lg_ref, lbias_ref, y_ref = tail

    parts = [(attn[j] * _silu(ga_ref[:, lanes].astype(F32))).astype(BF16) for j, lanes in enumerate(pairs)]
    for h in range(N_HEADS_B):
        hs = slice(h * HEAD_V_B, (h + 1) * HEAD_V_B)
        oh = ob_ref[:, hs].astype(F32)
        ms = jnp.mean(oh * oh, axis=-1, keepdims=True)
        on = oh * lax.rsqrt(ms + NORM_EPS) * ng_ref[...]
        parts.append((on * _silu(gb_ref[:, hs].astype(F32))).astype(BF16))
    mix = jnp.concatenate(parts, axis=-1)
    z = ALPHA * x_ref[...] + _dot(mix, wout_ref[...])
    mu = jnp.mean(z, axis=-1, keepdims=True)
    zc = z - mu
    var = jnp.mean(zc * zc, axis=-1, keepdims=True)
    y_ref[...] = zc * lax.rsqrt(var + NORM_EPS) * lg_ref[...] + lbias_ref[...]


def _merge(x2d, attn_outs, lses, ga, ob, gb, w_out_bf16, norm_g, ln_g, ln_b, tm, seq=None):
    n = x2d.shape[0]
    n_cfg = len(attn_outs)
    row = lambda width: pl.BlockSpec((tm, width), lambda i: (i, 0))
    const = lambda shape: pl.BlockSpec(shape, lambda i: (0,) * len(shape))
    if n_cfg > 1:
        assert tm == PROJ_TILE and n_cfg == len(DILATIONS)
        attn_specs = ([_blocked_tile_spec(seq, dil, D_A) for dil in DILATIONS]
                      + [_blocked_tile_spec(seq, dil, LANES) for dil in DILATIONS])
        scratch = []
        for dil in DILATIONS[1:] + (0,):
            scratch += [pltpu.VMEM((N_PAIRS, tm, LANES), F32), pltpu.VMEM((1, tm, LANES), F32)]
    else:
        attn_specs, scratch = [row(D_A)], []
    in_specs = ([row(D_MODEL)] + attn_specs
                + [row(D_A), row(D_B), row(D_B), const((D_MODEL, D_MODEL)),
                   const((1, HEAD_V_B)), const((1, D_MODEL)), const((1, D_MODEL))])
    return pl.pallas_call(
        functools.partial(_merge_body, n_cfg=n_cfg),
        grid=(n // tm,),
        in_specs=in_specs,
        out_specs=row(D_MODEL),
        out_shape=jax.ShapeDtypeStruct((n, D_MODEL), F32),
        scratch_shapes=scratch,
        compiler_params=_cparams(("arbitrary",), VMEM_MERGE_MIB),
        name=f"merge_{n_cfg}",
    )(x2d, *attn_outs, *lses, ga, ob, gb, w_out_bf16, norm_g, ln_g, ln_b)


def _bias_by_distance(rel_bias, dist):
    onehot = jax.nn.one_hot(_rel_buckets(dist.reshape(-1)), NUM_BUCKETS, dtype=F32)
    table = jnp.dot(onehot, rel_bias.astype(F32), precision=lax.Precision.HIGHEST)
    return table.reshape(dist.shape + (N_HEADS_A,))


def _prompt_bias(rel_bias, dil):
    i = np.arange(ATTN_BLOCK)[:, None]
    j = np.arange(2 * ATTN_BLOCK)[None, :]
    sub = ATTN_BLOCK + i - j
    ok = (sub >= 0) & (sub <= WIN_STEPS)
    bias = _bias_by_distance(rel_bias, dil * np.clip(sub, 0, WIN_STEPS)) * LOG2_E
    full = jnp.where(ok[:, :, None], bias, -jnp.inf)
    no_prev = jnp.where((ok & (j >= ATTN_BLOCK))[:, :, None], bias, -jnp.inf)
    return jnp.transpose(jnp.concatenate([full, no_prev], axis=2), (2, 0, 1))


def _to_feature_major(cache):
    b, p, h, d = cache.shape
    return jnp.transpose(cache, (0, 2, 3, 1)).reshape(b, h * d, p)


def _from_feature_major(cache_t):
    b, _, p = cache_t.shape
    return jnp.transpose(cache_t.reshape(b, N_HEADS_A, HEAD_DIM_A, p), (0, 3, 1, 2))


def kernel(x_prompt, x_sample, cache_k, cache_v, state_hgrn, w_in, w_out, rel_bias, lb_param, hgrn_norm_g, ln_g, ln_b):
    bsz, seq, _ = x_prompt.shape
    dec_b, dec_t, _ = x_sample.shape
    assert dec_t == DEC_SEQ and cache_k.shape[2] == WINDOW_MAX and w_in.shape[0] == DEPTH
    keep = min(WINDOW_MAX, seq)

    w_in16 = w_in[0].astype(BF16)
    w_out16 = w_out[0].astype(BF16)
    norm_g = hgrn_norm_g[0].reshape(1, HEAD_V_B).astype(F32)
    lng = ln_g[0].reshape(1, D_MODEL).astype(F32)
    lnb = ln_b[0].reshape(1, D_MODEL).astype(F32)
    lbp = lb_param.astype(F32)

    tm = PROJ_TILE
    n_p = bsz * seq
    x2d = x_prompt.reshape(n_p, D_MODEL)
    proj = _proj_prompt(x2d, w_in16, bsz, seq, keep)
    n_cfg = len(DILATIONS)
    qs, ks, vs = proj[0:n_cfg], proj[n_cfg:2 * n_cfg], proj[2 * n_cfg:3 * n_cfg]
    kwin, vwin, ga, qb, fb, ib, gb = proj[3 * n_cfg:]

    attn_outs, lses = [], []
    for ci, dil in enumerate(DILATIONS):
        o, lse = _attn_prompt(qs[ci], ks[ci], vs[ci], _prompt_bias(rel_bias, dil), dil)
        attn_outs.append(o)
        lses.append(lse)

    n_s = dec_b * dec_t
    xs2d = x_sample.reshape(n_s, D_MODEL)
    tm_s = min(tm, n_s)
    sqa, ska, sva, sga, sqb, sfb, sib, sgb = _proj(xs2d, w_in16, (D_A, D_A, D_A, D_A, D_F, D_F, D_B, D_B), tm_s)
    dist_s, masks_s = _sample_key_tables()
    base_s = jnp.transpose(_bias_by_distance(rel_bias, dist_s), (2, 0, 1))
    attn_s, k_new_t, v_new_t, ob, s_prompt = _attn_sample_hgrn_prompt(
        sqa, ska, sva, _to_feature_major(cache_k[0]), _to_feature_major(cache_v[0]), base_s, jnp.asarray(masks_s),
        qb, fb, ib, lbp, bsz, seq)
    y_prompt = _merge(x2d, attn_outs, lses, ga, ob, gb, w_out16, norm_g, lng, lnb, tm, seq).reshape(bsz, seq, D_MODEL)
    ob_s, s_sample = _hgrn_sample(sqb, sfb, sib, lbp, state_hgrn[0], LANES // DEC_SEQ)
    y_sample = _merge(xs2d, [attn_s], [], sga, ob_s, sgb, w_out16, norm_g, lng, lnb, tm_s).reshape(dec_b, dec_t, D_MODEL)

    return (y_prompt, y_sample,
            _from_feature_major(kwin)[None], _from_feature_major(vwin)[None],
            _from_feature_major(k_new_t)[None], _from_feature_major(v_new_t)[None],
            s_prompt[None], s_sample[None].astype(state_hgrn.dtype))
```

```python
import functools

import numpy as np
import jax
import jax.numpy as jnp
from jax import lax
from jax.experimental import pallas as pl
from jax.experimental.pallas import tpu as pltpu

F32 = jnp.float32
BF16 = jnp.bfloat16

D_MODEL = 1024
D_A = 512
HEAD_DIM_A = 64
N_HEADS_A = 8
DILATIONS = (1, 4, 16)
WIN_STEPS = 128
WINDOW_MAX = 2048
ATTN_BLOCK = 128
ATTN_SCALE = HEAD_DIM_A ** -0.5
NUM_BUCKETS = 32
REL_MAX_DIST = 2048
D_B = 512
N_HEADS_B = 4
HEAD_V_B = 128
EXPAND_B = 128
D_F = 512
GLA_CHUNK = 64
D_IN = 4 * D_A + 2 * D_F + 2 * D_B
DEPTH = 1
ALPHA = (2.0 * DEPTH) ** 0.25
NORM_EPS = 1e-5
DEC_SEQ = 8
LOG2_E = 1.4426950408889634
LN_2 = 0.6931471805599453

LANES = 128
MIB = 1024 * 1024
VMEM_LARGE_MIB = 56
VMEM_MERGE_MIB = 48
VMEM_ATTN_MIB = 40
VMEM_SMALL_MIB = 32

NT_DIMS = (((1,), (1,)), ((), ()))
TN_DIMS = (((0,), (0,)), ((), ()))


def _cparams(semantics, vmem_mib):
    return pltpu.CompilerParams(dimension_semantics=semantics, vmem_limit_bytes=vmem_mib * MIB)


def _rel_buckets(dist):
    max_exact = NUM_BUCKETS // 2
    d = np.maximum(dist, 1).astype(np.float32)
    large = max_exact + (np.log(d / max_exact) / np.log(REL_MAX_DIST / max_exact)
                         * (NUM_BUCKETS - max_exact)).astype(np.int32)
    large = np.minimum(large, NUM_BUCKETS - 1)
    return np.where(dist < max_exact, dist, large).astype(np.int32)


def _split3(x):
    hi = x.astype(BF16)
    r1 = x - hi.astype(F32)
    mid = r1.astype(BF16)
    lo = (r1 - mid.astype(F32)).astype(BF16)
    return hi, mid, lo


def _dot(a, b):
    return jnp.dot(a, b, preferred_element_type=F32)


def _dot3(mat_bf16, x):
    hi, mid, lo = _split3(x)
    return _dot(mat_bf16, hi) + _dot(mat_bf16, mid) + _dot(mat_bf16, lo)


def _split2(x):
    hi = x.astype(BF16)
    return hi, (x - hi.astype(F32)).astype(BF16)


def _dot2_lhs(mat_bf16, x):
    hi, lo = _split2(x)
    return _dot(mat_bf16, hi) + _dot(mat_bf16, lo)


def _sigmoid(x):
    return 0.5 * jnp.tanh(0.5 * x) + 0.5


def _silu(x):
    return x * _sigmoid(x)


def _bcast_rows(x, block, j):
    rows = x.shape[0]
    parts = [jnp.broadcast_to(x[r0 + j:r0 + j + 1, :], (block, x.shape[1])) for r0 in range(0, rows, block)]
    return parts[0] if len(parts) == 1 else jnp.concatenate(parts, axis=0)


def _proj_body(x_ref, w_ref, *out_refs, widths):
    x = x_ref[...].astype(BF16)
    lo = 0
    for ref, width in zip(out_refs, widths):
        ref[...] = _dot(x, w_ref[:, lo:lo + width])
        lo += width


def _proj(x2d, w_bf16, widths, tm):
    n = x2d.shape[0]
    return pl.pallas_call(
        functools.partial(_proj_body, widths=widths),
        grid=(n // tm,),
        in_specs=[pl.BlockSpec((tm, D_MODEL), lambda i: (i, 0)),
                  pl.BlockSpec((D_MODEL, D_IN), lambda i: (0, 0))],
        out_specs=[pl.BlockSpec((tm, width), lambda i: (i, 0)) for width in widths],
        out_shape=[jax.ShapeDtypeStruct((n, width), F32) for width in widths],
        compiler_params=_cparams(("arbitrary",), VMEM_LARGE_MIB),
        name="proj",
    )(x2d, w_bf16)


PROJ_TILE = 512
N_PAIRS = D_A // LANES


def _blocked_shape(bsz, seq, dil, width, dtype):
    return jax.ShapeDtypeStruct((bsz, dil, seq // (dil * ATTN_BLOCK), ATTN_BLOCK, width), dtype)


def _blocked_tile_spec(seq, dil, width):
    tiles = seq // PROJ_TILE
    rows = PROJ_TILE // dil
    if rows >= ATTN_BLOCK:
        return pl.BlockSpec((None, dil, rows // ATTN_BLOCK, ATTN_BLOCK, width),
                            lambda i: (i // tiles, 0, i % tiles, 0, 0))
    per = ATTN_BLOCK // rows
    return pl.BlockSpec((None, dil, None, rows, width),
                        lambda i: (i // tiles, 0, (i % tiles) // per, (i % tiles) % per, 0))


def _store_blocked(out_refs, slabs, slabs4):
    assert DILATIONS == (1, 4, 16) and PROJ_TILE == 4 * ATTN_BLOCK
    out4, out16 = out_refs
    sub = ATTN_BLOCK // 4
    for j in range(slabs.shape[0]):
        lanes = slice(LANES * j, LANES * (j + 1))
        for c in range(4):
            piece = slabs[j, pl.ds(c, ATTN_BLOCK, stride=4), :]
            slabs4[j, c * ATTN_BLOCK:(c + 1) * ATTN_BLOCK, :] = piece
            out4[c, 0, :, lanes] = piece.astype(out4.dtype)
        for c in range(4):
            for r in range(4):
                piece = slabs4[j, pl.ds(c * ATTN_BLOCK + r, sub, stride=4), :]
                out16[c + 4 * r, :, lanes] = piece.astype(out16.dtype)


def _load_blocked(in_ref, slabs, dil, slabs4=None):
    sub = ATTN_BLOCK // 4
    for j in range(slabs.shape[0]):
        lanes = slice(LANES * j, LANES * (j + 1))
        for c in range(4):
            if dil == 4:
                piece = in_ref[c, 0, :, lanes].astype(F32)
            else:
                for r in range(4):
                    slabs4[j, pl.ds(c * ATTN_BLOCK + r, sub, stride=4), :] = in_ref[c + 4 * r, :, lanes].astype(F32)
                piece = slabs4[j, c * ATTN_BLOCK:(c + 1) * ATTN_BLOCK, :]
            slabs[j, pl.ds(c, ATTN_BLOCK, stride=4), :] = piece


def _proj_prompt_body(x_ref, w_ref, *refs):
    n_cfg = len(DILATIONS)
    blocked = [refs[n_cfg * g:n_cfg * (g + 1)] for g in range(3)]
    kwin_ref, vwin_ref, ga_ref, qb_ref, fb_ref, ib_ref, gb_ref = refs[3 * n_cfg:3 * n_cfg + 7]
    x16_ref = refs[3 * n_cfg + 7]
    slab_sets = refs[3 * n_cfg + 8:3 * n_cfg + 11]
    slab4_sets = refs[3 * n_cfg + 11:]
    x16_ref[...] = x_ref[...].astype(BF16)
    for g, scale in enumerate((ATTN_SCALE * LOG2_E, 1.0, 1.0)):
        h = _dot(x16_ref[...], w_ref[:, D_A * g:D_A * (g + 1)])
        if scale != 1.0:
            h = h * scale
        for j in range(N_PAIRS):
            slab_sets[g][j] = h[:, LANES * j:LANES * (j + 1)]
        for k in range(PROJ_TILE // ATTN_BLOCK):
            blocked[g][0][0, k] = h[k * ATTN_BLOCK:(k + 1) * ATTN_BLOCK, :].astype(BF16)
    lo = 3 * D_A
    for ref, width in ((ga_ref, D_A), (qb_ref, D_F), (fb_ref, D_F), (ib_ref, D_B), (gb_ref, D_B)):
        ref[...] = _dot(x16_ref[...], w_ref[:, lo:lo + width]).astype(ref.dtype)
        lo += width
    for g, win_ref in enumerate((None, kwin_ref, vwin_ref)):
        if win_ref is not None:
            for j in range(N_PAIRS):
                win_ref[LANES * j:LANES * (j + 1), :] = slab_sets[g][j].T
        _store_blocked(blocked[g][1:], slab_sets[g], slab4_sets[g])


def _proj_prompt(x2d, w_bf16, bsz, seq, keep):
    n = bsz * seq
    tiles = seq // PROJ_TILE
    win_tiles = keep // PROJ_TILE
    row = lambda width: pl.BlockSpec((PROJ_TILE, width), lambda i: (i, 0))
    win = pl.BlockSpec((None, D_A, PROJ_TILE),
                       lambda i: (i // tiles, 0, jnp.maximum(i % tiles - (tiles - win_tiles), 0)))
    out_shape, out_specs = [], []
    for _ in range(3):
        for dil in DILATIONS:
            out_shape.append(_blocked_shape(bsz, seq, dil, D_A, BF16))
            out_specs.append(_blocked_tile_spec(seq, dil, D_A))
    for _ in range(2):
        out_shape.append(jax.ShapeDtypeStruct((bsz, D_A, keep), F32))
        out_specs.append(win)
    for width, dt in ((D_A, BF16), (D_F, F32), (D_F, F32), (D_B, F32), (D_B, BF16)):
        out_shape.append(jax.ShapeDtypeStruct((n, width), dt))
        out_specs.append(row(width))
    return pl.pallas_call(
        _proj_prompt_body,
        grid=(n // PROJ_TILE,),
        in_specs=[row(D_MODEL), pl.BlockSpec((D_MODEL, D_IN), lambda i: (0, 0))],
        out_specs=out_specs,
        out_shape=out_shape,
        scratch_shapes=([pltpu.VMEM((PROJ_TILE, D_MODEL), BF16)]
                        + [pltpu.VMEM((N_PAIRS, PROJ_TILE, LANES), F32) for _ in range(6)]),
        compiler_params=_cparams(("arbitrary",), VMEM_LARGE_MIB),
        name="proj_prompt",
    )(x2d, w_bf16)


UNITS_PER_STEP = 16


def _attn_unit(q, k_prev, k_cur, v_prev, v_cur, bias_ref, bias_set):
    lane = _iota2((ATTN_BLOCK, LANES), 1)
    low_half = lane < HEAD_DIM_A
    scores = []
    for j in range(N_PAIRS):
        lanes = slice(LANES * j, LANES * (j + 1))
        qj = q[:, lanes]
        zero = jnp.zeros_like(qj)
        kcat = jnp.concatenate([k_prev[:, lanes], k_cur[:, lanes]], axis=0)
        for qh in (jnp.where(low_half, qj, zero), jnp.where(low_half, zero, qj)):
            scores.append(lax.dot_general(qh, kcat, NT_DIMS, preferred_element_type=F32))

    probs, den_pairs = [], []
    m_tile = jnp.zeros((ATTN_BLOCK, LANES), F32)
    den_tile = jnp.ones((ATTN_BLOCK, LANES), F32)
    for h in range(N_HEADS_A):
        s = scores[h] + bias_ref[bias_set + h]
        m = jnp.max(jnp.maximum(s[:, :ATTN_BLOCK], s[:, ATTN_BLOCK:]), axis=1, keepdims=True)
        p = jnp.exp2(s - m)
        den = jnp.sum(p[:, :ATTN_BLOCK] + p[:, ATTN_BLOCK:], axis=1, keepdims=True)
        probs.append(p.astype(BF16))
        m_tile = jnp.where(lane == h, m, m_tile)
        den_tile = jnp.where(lane == h, den, den_tile)
        if h % 2 == 0:
            den_even = den
        else:
            den_pairs.append(jnp.where(low_half, den_even, den))
    lse_tile = (m_tile + jnp.log2(den_tile)) * LN_2

    outs = []
    for j in range(N_PAIRS):
        lanes = slice(LANES * j, LANES * (j + 1))
        vcat = jnp.concatenate([v_prev[:, lanes], v_cur[:, lanes]], axis=0)
        o_pair = jnp.where(low_half, _dot(probs[2 * j], vcat), _dot(probs[2 * j + 1], vcat))
        outs.append(o_pair / den_pairs[j])
    return outs, lse_tile


def _attn_prompt_body(q_ref, k_ref, v_ref, bias_ref, o_ref, lse_ref, kc_ref, vc_ref, *, n_res, n_blk):
    step = pl.program_id(2)
    slot = lax.rem(step, 2)

    @pl.when(step == 0)
    def _():
        kc_ref[0] = jnp.zeros_like(kc_ref[0])
        vc_ref[0] = jnp.zeros_like(vc_ref[0])

    def run(rr, n, k_prev, v_prev, bias_set):
        outs, lse_tile = _attn_unit(q_ref[rr, n], k_prev, k_ref[rr, n], v_prev, v_ref[rr, n], bias_ref, bias_set)
        for j in range(N_PAIRS):
            o_ref[rr, n, :, LANES * j:LANES * (j + 1)] = outs[j].astype(o_ref.dtype)
        lse_ref[rr, n] = lse_tile

    start_set = jnp.where(step == 0, N_HEADS_A, 0)
    for rr in range(n_res):
        run(rr, 0, kc_ref[slot, rr], vc_ref[slot, rr], start_set)

        def later(n, carry, rr=rr):
            run(rr, n, k_ref[rr, n - 1], v_ref[rr, n - 1], 0)
            return carry

        lax.fori_loop(1, n_blk, later, 0, unroll=True)
        kc_ref[1 - slot, rr] = k_ref[rr, n_blk - 1]
        vc_ref[1 - slot, rr] = v_ref[rr, n_blk - 1]


def _attn_prompt(q, k, v, bias, dil):
    b, _, nb, _, _ = q.shape
    n_blk = min(nb, UNITS_PER_STEP)
    n_res = min(dil, UNITS_PER_STEP // n_blk)
    blk = lambda width: pl.BlockSpec((None, n_res, n_blk, ATTN_BLOCK, width), lambda bi, ri, si: (bi, ri, si, 0, 0))
    return pl.pallas_call(
        functools.partial(_attn_prompt_body, n_res=n_res, n_blk=n_blk),
        grid=(b, dil // n_res, nb // n_blk),
        in_specs=[blk(D_A), blk(D_A), blk(D_A),
                  pl.BlockSpec((2 * N_HEADS_A, ATTN_BLOCK, 2 * ATTN_BLOCK), lambda bi, ri, si: (0, 0, 0))],
        out_specs=[blk(D_A), blk(LANES)],
        out_shape=[jax.ShapeDtypeStruct(q.shape, BF16),
                   jax.ShapeDtypeStruct(q.shape[:-1] + (LANES,), F32)],
        scratch_shapes=[pltpu.VMEM((2, n_res, ATTN_BLOCK, D_A), BF16), pltpu.VMEM((2, n_res, ATTN_BLOCK, D_A), BF16)],
        compiler_params=_cparams(("arbitrary", "arbitrary", "arbitrary"), VMEM_ATTN_MIB),
        name=f"attn_prompt_d{dil}",
    )(q, k, v, bias)


def _lower_bound(lbp_ref):
    p0 = lbp_ref[0:1, :]
    p1 = lbp_ref[1:2, :]
    m = jnp.maximum(p0, p1)
    e0 = jnp.exp(p0 - m)
    e1 = jnp.exp(p1 - m)
    return e0 / (e0 + e1)


def _gates(qb, fb, lb):
    f = lb + (1.0 - lb) * _sigmoid(fb)
    return _silu(qb) * (EXPAND_B ** -0.5), 1.0 - f, jnp.log(f)


def _iota2(shape, dim):
    return lax.broadcasted_iota(jnp.int32, shape, dim)


def _div2(x, n):
    return jnp.right_shift(x, int(n).bit_length() - 1)


def _mod2(x, n):
    return jnp.bitwise_and(x, n - 1)


def _head_expand_matrix():
    return (_iota2((LANES, D_A), 0) == _div2(_iota2((LANES, D_A), 1), HEAD_DIM_A)).astype(BF16)


def _diag_block_scores(q, kk, b, a_heads, row, col, block):
    same = _div2(row, block) == _div2(col, block)
    for j in range(block):
        bj = _bcast_rows(b, block, j)
        kj = _bcast_rows(kk, block, j)
        x = q * jnp.exp(b - bj) * kj
        sel = same & (_mod2(col, block) == j) & (_mod2(row, block) >= j)
        for h in range(N_HEADS_B):
            cs = jnp.sum(x[:, h * EXPAND_B:(h + 1) * EXPAND_B], axis=-1, keepdims=True)
            a_heads[h] = jnp.where(sel, cs, a_heads[h])
    return a_heads


def _hgrn_prompt_tile(qb_ref, fb_ref, ib_ref, lbp_ref, o_ref, st_ref, q_s, kk_s, b_s, ref_s, qin_s, kout_s, v_s, a_s,
                      *, n_chunks):
    c_len = GLA_CHUNK
    chunks = [slice(c * c_len, (c + 1) * c_len) for c in range(n_chunks)]
    heads = [slice(h * EXPAND_B, (h + 1) * EXPAND_B) for h in range(N_HEADS_B)]

    lb = _lower_bound(lbp_ref)
    row = _iota2((c_len, c_len), 0)
    col = _iota2((c_len, c_len), 1)
    tri = (col <= row).astype(BF16)
    half_sizes = (32, 16, 8, 4, 2, 1)
    masks, picks = [], []
    for n in half_sizes:
        masks.append((_div2(row, 2 * n) == _div2(col, 2 * n)) & (_mod2(_div2(row, n), 2) == 1)
                     & (_mod2(_div2(col, n), 2) == 0))
        picks.append((col == _div2(row, 2 * n) * (2 * n) + (n - 1)).astype(BF16))
    pick_all = jnp.concatenate(picks, axis=0)
    on_diag = row == col

    for c, cs in enumerate(chunks):
        q, kk, g = _gates(qb_ref[cs, :], fb_ref[cs, :], lb)
        q_s[cs, :] = q
        kk_s[cs, :] = kk
        v_s[cs, :] = ib_ref[cs, :].astype(BF16)
        b_c = _dot2_lhs(tri, g)
        b_s[cs, :] = b_c
        refs = _dot(pick_all, b_c.astype(BF16))
        for li in range(len(half_sizes)):
            ref_s[li, cs, :] = refs[li * c_len:(li + 1) * c_len, :]
        b_end = b_c[c_len - 1:c_len, :]
        qin_s[cs, :] = (q * jnp.exp(b_c)).astype(BF16)
        kout_s[cs, :] = (kk * jnp.exp(b_end - b_c)).astype(BF16)
        qk = q * kk
        for h, hs in enumerate(heads):
            a_s[c, h] = jnp.where(on_diag, jnp.sum(qk[:, hs], axis=-1, keepdims=True), 0.0)
    for li, mask in enumerate(masks):
        for c, cs in enumerate(chunks):
            d = b_s[cs, :] - ref_s[li, cs, :]
            ql = (q_s[cs, :] * jnp.exp(d)).astype(BF16)
            kl = (kk_s[cs, :] * jnp.exp(-d)).astype(BF16)
            for h, hs in enumerate(heads):
                part = lax.dot_general(ql[:, hs], kl[:, hs], NT_DIMS, preferred_element_type=F32)
                a_s[c, h] = jnp.where(mask, part, a_s[c, h])

    for c, cs in enumerate(chunks):
        dec = jnp.exp(b_s[c * c_len + c_len - 1:(c + 1) * c_len, :])
        for h, hs in enumerate(heads):
            state = st_ref[h]
            inter = lax.dot_general(qin_s[cs, hs], state.astype(BF16), NT_DIMS, preferred_element_type=F32)
            intra = _dot(a_s[c, h].astype(BF16), v_s[cs, hs])
            o_ref[cs, hs] = (inter + intra).astype(o_ref.dtype)
            upd = lax.dot_general(v_s[cs, hs], kout_s[cs, hs], TN_DIMS, preferred_element_type=F32)
            st_ref[h] = state * dec[:, hs] + upd


def _hgrn_sample_body(qb_ref, fb_ref, ib_ref, lbp_ref, s0_ref, o_ref, s_ref, *, nbatch):
    rows_n = nbatch * DEC_SEQ
    assert rows_n == LANES
    lb = _lower_bound(lbp_ref)
    row = _iota2((rows_n, rows_n), 0)
    col = _iota2((rows_n, rows_n), 1)
    same = _div2(row, DEC_SEQ) == _div2(col, DEC_SEQ)
    tri = (same & (col <= row)).astype(BF16)
    pick = (col == row * DEC_SEQ + (DEC_SEQ - 1)).astype(BF16)

    q, kk, g = _gates(qb_ref[...], fb_ref[...], lb)
    v = ib_ref[...]
    b = _dot3(tri, g)
    b_end = _bcast_rows(b, DEC_SEQ, DEC_SEQ - 1)
    q_in = (q * jnp.exp(b)).astype(BF16)
    k_out = kk * jnp.exp(b_end - b)
    v16 = v.astype(BF16)

    a_heads = [jnp.zeros((rows_n, rows_n), F32) for _ in range(N_HEADS_B)]
    a_heads = _diag_block_scores(q, kk, b, a_heads, row, col, DEC_SEQ)

    rsel = _div2(_iota2((rows_n, 1), 0), DEC_SEQ)
    for h in range(N_HEADS_B):
        hs = slice(h * EXPAND_B, (h + 1) * EXPAND_B)
        intra = _dot(a_heads[h].astype(BF16), v16[:, hs])
        dec_t = jnp.exp(_dot3(pick, b[:, hs])).T
        k_out_t = k_out[:, hs].T.astype(BF16)
        for bi in range(nbatch):
            rs = slice(bi * DEC_SEQ, (bi + 1) * DEC_SEQ)
            s0 = s0_ref[bi, h]
            inter = _dot(q_in[rs, hs], s0.astype(BF16))
            o_ref[rs, hs] = inter + intra[rs, :]
            v_b = jnp.where(rsel == bi, v16[:, hs], jnp.zeros_like(v16[:, hs]))
            upd = _dot(k_out_t, v_b)
            s_ref[bi, h] = s0 * dec_t[:, bi:bi + 1] + upd


def _hgrn_sample(qb, fb, ib, lb_param, s0, nbatch):
    b = s0.shape[0]
    rows_n = nbatch * DEC_SEQ
    blk = pl.BlockSpec((rows_n, D_F), lambda i: (i, 0))
    sblk = pl.BlockSpec((nbatch, N_HEADS_B, EXPAND_B, HEAD_V_B), lambda i: (i, 0, 0, 0))
    return pl.pallas_call(
        functools.partial(_hgrn_sample_body, nbatch=nbatch),
        grid=(b // nbatch,),
        in_specs=[blk, blk, blk, pl.BlockSpec((DEPTH + 1, D_F), lambda i: (0, 0)), sblk],
        out_specs=[blk, sblk],
        out_shape=[jax.ShapeDtypeStruct((b * DEC_SEQ, D_B), F32),
                   jax.ShapeDtypeStruct(s0.shape, F32)],
        compiler_params=_cparams(("arbitrary",), VMEM_SMALL_MIB),
        name="hgrn_sample",
    )(qb, fb, ib, lb_param, s0)


EXT_KEYS = WINDOW_MAX + LANES
SHIFT_ROWS = 64


def _sample_key_tables():
    t = np.arange(DEC_SEQ)[:, None]
    e = np.arange(EXT_KEYS)[None, :]
    is_new = e >= WINDOW_MAX
    d = np.where(is_new, t - (e - WINDOW_MAX), WINDOW_MAX + t - e)
    masks = np.full((len(DILATIONS), DEC_SEQ, EXT_KEYS), -np.inf, np.float32)
    for ci, dil in enumerate(DILATIONS):
        ok = (d >= 0) & (d % dil == 0) & (d <= WIN_STEPS * dil) & (e < WINDOW_MAX + DEC_SEQ)
        masks[ci][ok] = 0.0
    return np.maximum(d, 0), masks


def _window_shift(kn_ref, vn_ref, ck_ref, cv_ref, ok_ref, ov_ref):
    keep = LANES - DEC_SEQ
    lane = _iota2((SHIFT_ROWS, LANES), 1)
    pad_rows = jnp.zeros((keep, D_A), F32)
    for c_ref, n_ref, o_ref in ((ck_ref, kn_ref, ok_ref), (cv_ref, vn_ref, ov_ref)):
        new_t = jnp.concatenate([pad_rows, n_ref[...]], axis=0).T
        for r0 in range(0, D_A, SHIFT_ROWS):
            rows = slice(r0, r0 + SHIFT_ROWS)
            rolled = pltpu.roll(c_ref[rows, :], WINDOW_MAX - DEC_SEQ, axis=1)
            o_ref[rows, 0:WINDOW_MAX - LANES] = rolled[:, 0:WINDOW_MAX - LANES]
            o_ref[rows, WINDOW_MAX - LANES:] = jnp.where(lane >= keep, new_t[rows, :], rolled[:, WINDOW_MAX - LANES:])


def _sample_heads(q_ref, kn_ref, vn_ref, ck_ref, cv_ref, base_ref, mask_ref, attn_ref):
    keep = LANES - DEC_SEQ
    rows = N_HEADS_A * DEC_SEQ
    own = _div2(_iota2((rows, D_A), 0), DEC_SEQ) == _div2(_iota2((rows, D_A), 1), HEAD_DIM_A)
    qs = q_ref[...] * ATTN_SCALE
    q_all = jnp.where(own, jnp.concatenate([qs] * N_HEADS_A, axis=0), 0.0).astype(BF16)
    pad_new = jnp.zeros((keep, D_A), F32)
    kn_pad = jnp.concatenate([kn_ref[...], pad_new], axis=0).astype(BF16)
    vn_pad = jnp.concatenate([vn_ref[...], pad_new], axis=0).astype(BF16)
    s_cache = _dot(q_all, ck_ref[...].astype(BF16))
    s_new = lax.dot_general(q_all, kn_pad, NT_DIMS, preferred_element_type=F32)
    s_all = jnp.concatenate([s_cache, s_new], axis=1)
    pws = []
    for h in range(N_HEADS_A):
        s = s_all[DEC_SEQ * h:DEC_SEQ * (h + 1), :] + base_ref[h]
        probs, lses = [], []
        for ci in range(len(DILATIONS)):
            sc = s + mask_ref[ci]
            m = jnp.max(sc, axis=1, keepdims=True)
            p = jnp.exp(sc - m)
            den = jnp.sum(p, axis=1, keepdims=True)
            probs.append(p / den)
            lses.append(m + jnp.log(den))
        lmax = jnp.maximum(jnp.maximum(lses[0], lses[1]), lses[2])
        ws = [jnp.exp(l - lmax) for l in lses]
        wsum = ws[0] + ws[1] + ws[2]
        pws.append(probs[0] * (ws[0] / wsum) + probs[1] * (ws[1] / wsum) + probs[2] * (ws[2] / wsum))
    pw = jnp.concatenate(pws, axis=0).astype(BF16)
    o = lax.dot_general(pw[:, 0:WINDOW_MAX], cv_ref[...].astype(BF16), NT_DIMS, preferred_element_type=F32)
    o = jnp.where(own, o + _dot(pw[:, WINDOW_MAX:], vn_pad), 0.0)
    acc = o[0:DEC_SEQ, :]
    for h in range(1, N_HEADS_A):
        acc = acc + o[DEC_SEQ * h:DEC_SEQ * (h + 1), :]
    attn_ref[...] = acc


N_ATTN_SAMPLE_IN = 7
N_ATTN_SAMPLE_OUT = 3


def _attn_sample_hgrn_prompt_body(*refs, n_chunks, tiles_per_seq):
    n_in = N_ATTN_SAMPLE_IN + 4
    n_out = N_ATTN_SAMPLE_OUT + 2
    ins, outs, scratch = refs[:n_in], refs[n_in:n_in + n_out], refs[n_in + n_out:]
    o_ref, s_ref = outs[N_ATTN_SAMPLE_OUT:]
    kst_ref, vst_ref, sem = scratch[-3:]
    scratch = scratch[:-3]
    st_ref = scratch[0]
    step = pl.program_id(0)
    n_steps = pl.num_programs(0)
    slot = lax.rem(step, 2)
    t = lax.rem(step, tiles_per_seq)
    attn_ref, ok_hbm, ov_hbm = outs[:N_ATTN_SAMPLE_OUT]

    def window_copies(s, dst):
        return (pltpu.make_async_copy(kst_ref.at[s], ok_hbm.at[dst], sem.at[0, s]),
                pltpu.make_async_copy(vst_ref.at[s], ov_hbm.at[dst], sem.at[1, s]))

    @pl.when(step >= 2)
    def _():
        for cp in window_copies(slot, step - 2):
            cp.wait()

    @pl.when(t == 0)
    def _():
        st_ref[...] = jnp.zeros_like(st_ref)

    q_ref, kn_ref, vn_ref, ck_ref, cv_ref, base_ref, mask_ref = ins[:N_ATTN_SAMPLE_IN]
    _sample_heads(q_ref, kn_ref, vn_ref, ck_ref, cv_ref, base_ref, mask_ref, attn_ref)
    _hgrn_prompt_tile(*ins[N_ATTN_SAMPLE_IN:], o_ref, *scratch, n_chunks=n_chunks)
    _window_shift(kn_ref, vn_ref, ck_ref, cv_ref, kst_ref.at[slot], vst_ref.at[slot])
    for cp in window_copies(slot, step):
        cp.start(priority=1)

    @pl.when(step == n_steps - 1)
    def _():
        for cp in window_copies(slot, step):
            cp.wait()

    @pl.when(jnp.logical_and(step == n_steps - 1, step >= 1))
    def _():
        for cp in window_copies(1 - slot, step - 1):
            cp.wait()

    @pl.when(t == tiles_per_seq - 1)
    def _():
        for h in range(N_HEADS_B):
            s_ref[h] = st_ref[h].T


def _attn_sample_hgrn_prompt(q, kn, vn, cache_kt, cache_vt, base, masks, qb, fb, ib, lb_param, bsz, seq):
    dec_b = cache_kt.shape[0]
    tt = bsz * seq // dec_b
    assert bsz * seq == tt * dec_b and seq % tt == 0 and tt % GLA_CHUNK == 0
    n_chunks = tt // GLA_CHUNK
    small = pl.BlockSpec((DEC_SEQ, D_A), lambda i: (i, 0))
    big = pl.BlockSpec((None, D_A, WINDOW_MAX), lambda i: (i, 0, 0))
    const = lambda a: pl.BlockSpec(a.shape, lambda i: (0,) * a.ndim)
    blk = pl.BlockSpec((tt, D_F), lambda i: (i, 0))
    state_blk = pl.BlockSpec((None, N_HEADS_B, EXPAND_B, HEAD_V_B), lambda i: (i // (seq // tt), 0, 0, 0))
    tile = lambda dt: pltpu.VMEM((tt, D_F), dt)
    return pl.pallas_call(
        functools.partial(_attn_sample_hgrn_prompt_body, n_chunks=n_chunks, tiles_per_seq=seq // tt),
        grid=(dec_b,),
        in_specs=[small, small, small, big, big, const(base), const(masks), blk, blk, blk, const(lb_param)],
        out_specs=[small, pl.BlockSpec(memory_space=pl.ANY), pl.BlockSpec(memory_space=pl.ANY), blk, state_blk],
        out_shape=[jax.ShapeDtypeStruct((dec_b * DEC_SEQ, D_A), F32),
                   jax.ShapeDtypeStruct(cache_kt.shape, F32),
                   jax.ShapeDtypeStruct(cache_vt.shape, F32),
                   jax.ShapeDtypeStruct((bsz * seq, D_B), BF16),
                   jax.ShapeDtypeStruct((bsz, N_HEADS_B, EXPAND_B, HEAD_V_B), F32)],
        scratch_shapes=[pltpu.VMEM((N_HEADS_B, HEAD_V_B, EXPAND_B), F32),
                        tile(F32), tile(F32), tile(F32),
                        pltpu.VMEM((6, tt, D_F), F32),
                        tile(BF16), tile(BF16), tile(BF16),
                        pltpu.VMEM((n_chunks, N_HEADS_B, GLA_CHUNK, GLA_CHUNK), F32),
                        pltpu.VMEM((2, D_A, WINDOW_MAX), F32),
                        pltpu.VMEM((2, D_A, WINDOW_MAX), F32),
                        pltpu.SemaphoreType.DMA((2, 2))],
        compiler_params=_cparams(("arbitrary",), VMEM_LARGE_MIB),
        name="attn_sample_hgrn_prompt",
    )(q, kn, vn, cache_kt, cache_vt, base, masks, qb, fb, ib, lb_param)


def _merge_body(*refs, n_cfg):
    x_ref = refs[0]
    pairs = [slice(LANES * j, LANES * (j + 1)) for j in range(N_PAIRS)]
    if n_cfg > 1:
        o_refs = refs[1:1 + n_cfg]
        l_refs = refs[1 + n_cfg:1 + 2 * n_cfg]
        tail = refs[1 + 2 * n_cfg:9 + 2 * n_cfg]
        scratch = refs[9 + 2 * n_cfg:]
        o_nat, l_nat = [], []
        for ci, dil in enumerate(DILATIONS):
            if dil == 1:
                blocks = range(PROJ_TILE // ATTN_BLOCK)
                o_val = jnp.concatenate([o_refs[ci][0, k] for k in blocks], axis=0).astype(F32)
                o_nat.append([o_val[:, lanes] for lanes in pairs])
                l_nat.append(jnp.concatenate([l_refs[ci][0, k] for k in blocks], axis=0))
            else:
                o_slabs, l_slab = scratch[2 * (ci - 1)], scratch[2 * (ci - 1) + 1]
                o_tmp, l_tmp = scratch[-2:]
                _load_blocked(o_refs[ci], o_slabs, dil, o_tmp)
                _load_blocked(l_refs[ci], l_slab, dil, l_tmp)
                o_nat.append([o_slabs[j] for j in range(N_PAIRS)])
                l_nat.append(l_slab[0])
        et_mat = _head_expand_matrix()
        lmax = functools.reduce(jnp.maximum, l_nat)
        ws = [jnp.exp(l - lmax) for l in l_nat]
        wsum = functools.reduce(lambda a, c: a + c, ws)
        attn = [None] * N_PAIRS
        for w, o_c in zip(ws, o_nat):
            w_lanes = _dot((w / wsum).astype(BF16), et_mat)
            for j, lanes in enumerate(pairs):
                term = w_lanes[:, lanes] * o_c[j]
                attn[j] = term if attn[j] is None else attn[j] + term
    else:
        tail = refs[2:10]
        attn = [refs[1][:, lanes].astype(F32) for lanes in pairs]
    ga_ref, ob_ref, gb_ref, wout_ref, ng_ref, lg_ref, lbias_ref, y_ref = tail

    parts = [(attn[j] * _silu(ga_ref[:, lanes].astype(F32))).astype(BF16) for j, lanes in enumerate(pairs)]
    for h in range(N_HEADS_B):
        hs = slice(h * HEAD_V_B, (h + 1) * HEAD_V_B)
        oh = ob_ref[:, hs].astype(F32)
        ms = jnp.mean(oh * oh, axis=-1, keepdims=True)
        on = oh * lax.rsqrt(ms + NORM_EPS) * ng_ref[...]
        parts.append((on * _silu(gb_ref[:, hs].astype(F32))).astype(BF16))
    mix = jnp.concatenate(parts, axis=-1)
    z = ALPHA * x_ref[...] + _dot(mix, wout_ref[...])
    mu = jnp.mean(z, axis=-1, keepdims=True)
    zc = z - mu
    var = jnp.mean(zc * zc, axis=-1, keepdims=True)
    y_ref[...] = zc * lax.rsqrt(var + NORM_EPS) * lg_ref[...] + lbias_ref[...]


def _merge(x2d, attn_outs, lses, ga, ob, gb, w_out_bf16, norm_g, ln_g, ln_b, tm, seq=None):
    n = x2d.shape[0]
    n_cfg = len(attn_outs)
    row = lambda width: pl.BlockSpec((tm, width), lambda i: (i, 0))
    const = lambda shape: pl.BlockSpec(shape, lambda i: (0,) * len(shape))
    if n_cfg > 1:
        assert tm == PROJ_TILE and n_cfg == len(DILATIONS)
        attn_specs = ([_blocked_tile_spec(seq, dil, D_A) for dil in DILATIONS]
                      + [_blocked_tile_spec(seq, dil, LANES) for dil in DILATIONS])
        scratch = []
        for dil in DILATIONS[1:] + (0,):
            scratch += [pltpu.VMEM((N_PAIRS, tm, LANES), F32), pltpu.VMEM((1, tm, LANES), F32)]
    else:
        attn_specs, scratch = [row(D_A)], []
    in_specs = ([row(D_MODEL)] + attn_specs
                + [row(D_A), row(D_B), row(D_B), const((D_MODEL, D_MODEL)),
                   const((1, HEAD_V_B)), const((1, D_MODEL)), const((1, D_MODEL))])
    return pl.pallas_call(
        functools.partial(_merge_body, n_cfg=n_cfg),
        grid=(n // tm,),
        in_specs=in_specs,
        out_specs=row(D_MODEL),
        out_shape=jax.ShapeDtypeStruct((n, D_MODEL), F32),
        scratch_shapes=scratch,
        compiler_params=_cparams(("arbitrary",), VMEM_MERGE_MIB),
        name=f"merge_{n_cfg}",
    )(x2d, *attn_outs, *lses, ga, ob, gb, w_out_bf16, norm_g, ln_g, ln_b)


def _bias_by_distance(rel_bias, dist):
    onehot = jax.nn.one_hot(_rel_buckets(dist.reshape(-1)), NUM_BUCKETS, dtype=F32)
    table = jnp.dot(onehot, rel_bias.astype(F32), precision=lax.Precision.HIGHEST)
    return table.reshape(dist.shape + (N_HEADS_A,))


def _prompt_bias(rel_bias, dil):
    i = np.arange(ATTN_BLOCK)[:, None]
    j = np.arange(2 * ATTN_BLOCK)[None, :]
    sub = ATTN_BLOCK + i - j
    ok = (sub >= 0) & (sub <= WIN_STEPS)
    bias = _bias_by_distance(rel_bias, dil * np.clip(sub, 0, WIN_STEPS)) * LOG2_E
    full = jnp.where(ok[:, :, None], bias, -jnp.inf)
    no_prev = jnp.where((ok & (j >= ATTN_BLOCK))[:, :, None], bias, -jnp.inf)
    return jnp.transpose(jnp.concatenate([full, no_prev], axis=2), (2, 0, 1))


def _to_feature_major(cache):
    b, p, h, d = cache.shape
    return jnp.transpose(cache, (0, 2, 3, 1)).reshape(b, h * d, p)


def _from_feature_major(cache_t):
    b, _, p = cache_t.shape
    return jnp.transpose(cache_t.reshape(b, N_HEADS_A, HEAD_DIM_A, p), (0, 3, 1, 2))


def kernel(x_prompt, x_sample, cache_k, cache_v, state_hgrn, w_in, w_out, rel_bias, lb_param, hgrn_norm_g, ln_g, ln_b):
    bsz, seq, _ = x_prompt.shape
    dec_b, dec_t, _ = x_sample.shape
    assert dec_t == DEC_SEQ and cache_k.shape[2] == WINDOW_MAX and w_in.shape[0] == DEPTH
    keep = min(WINDOW_MAX, seq)

    w_in16 = w_in[0].astype(BF16)
    w_out16 = w_out[0].astype(BF16)
    norm_g = hgrn_norm_g[0].reshape(1, HEAD_V_B).astype(F32)
    lng = ln_g[0].reshape(1, D_MODEL).astype(F32)
    lnb = ln_b[0].reshape(1, D_MODEL).astype(F32)
    lbp = lb_param.astype(F32)

    tm = PROJ_TILE
    n_p = bsz * seq
    x2d = x_prompt.reshape(n_p, D_MODEL)
    proj = _proj_prompt(x2d, w_in16, bsz, seq, keep)
    n_cfg = len(DILATIONS)
    qs, ks, vs = proj[0:n_cfg], proj[n_cfg:2 * n_cfg], proj[2 * n_cfg:3 * n_cfg]
    kwin, vwin, ga, qb, fb, ib, gb = proj[3 * n_cfg:]

    attn_outs, lses = [], []
    for ci, dil in enumerate(DILATIONS):
        o, lse = _attn_prompt(qs[ci], ks[ci], vs[ci], _prompt_bias(rel_bias, dil), dil)
        attn_outs.append(o)
        lses.append(lse)

    n_s = dec_b * dec_t
    xs2d = x_sample.reshape(n_s, D_MODEL)
    tm_s = min(tm, n_s)
    sqa, ska, sva, sga, sqb, sfb, sib, sgb = _proj(xs2d, w_in16, (D_A, D_A, D_A, D_A, D_F, D_F, D_B, D_B), tm_s)
    dist_s, masks_s = _sample_key_tables()
    base_s = jnp.transpose(_bias_by_distance(rel_bias, dist_s), (2, 0, 1))
    attn_s, k_new_t, v_new_t, ob, s_prompt = _attn_sample_hgrn_prompt(
        sqa, ska, sva, _to_feature_major(cache_k[0]), _to_feature_major(cache_v[0]), base_s, jnp.asarray(masks_s),
        qb, fb, ib, lbp, bsz, seq)
    y_prompt = _merge(x2d, attn_outs, lses, ga, ob, gb, w_out16, norm_g, lng, lnb, tm, seq).reshape(bsz, seq, D_MODEL)
    ob_s, s_sample = _hgrn_sample(sqb, sfb, sib, lbp, state_hgrn[0], LANES // DEC_SEQ)
    y_sample = _merge(xs2d, [attn_s], [], sga, ob_s, sgb, w_out16, norm_g, lng, lnb, tm_s).reshape(dec_b, dec_t, D_MODEL)

    return (y_prompt, y_sample,
            _from_feature_major(kwin)[None], _from_feature_major(vwin)[None],
            _from_feature_major(k_new_t)[None], _from_feature_major(v_new_t)[None],
            s_prompt[None], s_sample[None].astype(state_hgrn.dtype))
```
